```python
import math
import numpy as np
import jax
import jax.numpy as jnp
from jax import lax

D_MODEL = 4096
BATCH = 2
SEQ = 4096
DEPTH = 1
DEC_BATCH = 16
DEC_SEQ = 32
PAST_LEN = 2048

CHUNK = 64
N_HEADS = 32
N_KV_HEADS = 8
HEAD_DIM = 128
ATTN_W = N_HEADS * HEAD_DIM
KV_W = N_KV_HEADS * HEAD_DIM
ROT_DIM = HEAD_DIM // 4
ROPE_THETA = 500000.0
IDX_HEADS = 32
IDX_DIM = 128
IDX_ROT_DIM = IDX_DIM // 4
TOPK_MAX = 256
Q_BLOCK = 128
LRU_W = D_MODEL
LRU_BLOCKS = 16
LRU_BW = LRU_W // LRU_BLOCKS
CONV_W = 4
LRU_C = 8.0
D_FF = -(-8 * D_MODEL // (3 * 256)) * 256
EPS = 1e-6
NEG = -1e30

IN_SIZES = (ATTN_W, KV_W, KV_W, IDX_HEADS * IDX_DIM, IDX_HEADS, IDX_DIM, LRU_W, LRU_W, D_MODEL, D_MODEL)
IN_SPLITS = tuple(int(s) for s in np.cumsum(IN_SIZES)[:-1])
IN_COLS = int(sum(IN_SIZES))

kernel_name = 'dsa_rglru_gated_hybrid_stream_step'


def rmsnorm(x, g):
    xf = x.astype(jnp.float32)
    y = xf * lax.rsqrt(jnp.mean(xf * xf, axis=-1, keepdims=True) + EPS) * g.astype(jnp.float32)
    return y.astype(x.dtype)


def rope_partial(x, pos, rot):
    half = rot // 2
    inv = ROPE_THETA ** (-jnp.arange(half, dtype=jnp.float32) * 2.0 / rot)
    ang = pos.astype(jnp.float32)[:, None] * inv[None, :]
    cos = jnp.cos(ang)[:, None, :]
    sin = jnp.sin(ang)[:, None, :]
    xf = x.astype(jnp.float32)
    x1 = xf[..., :half]
    x2 = xf[..., half:rot]
    out = jnp.concatenate([x1 * cos - x2 * sin, x2 * cos + x1 * sin, xf[..., rot:]], axis=-1)
    return out.astype(x.dtype)


def sparse_attend(q, iq, iw, q_pos, k_all, v_all, ik_all, k_pos, topk):
    f32 = jnp.float32
    B, Tq = q.shape[0], q.shape[1]
    s_idx = jnp.einsum('bthd,bsd->bths', iq.astype(f32), ik_all.astype(f32))
    s_idx = jnp.einsum('bths,bth->bts', jax.nn.relu(s_idx), iw.astype(f32))
    allowed = (k_pos[None, :] // CHUNK) <= (q_pos[:, None] // CHUNK)
    s_idx = jnp.where(allowed[None], s_idx, NEG)
    vals, sel = lax.top_k(s_idx, topk)
    valid = vals > 0.5 * NEG
    gather = jax.vmap(lambda kv, ix: kv[ix])
    k_sel = gather(k_all, sel)
    v_sel = gather(v_all, sel)
    qg = q.reshape(B, Tq, N_KV_HEADS, N_HEADS // N_KV_HEADS, HEAD_DIM).astype(f32)
    logits = jnp.einsum('btkgd,btskd->btkgs', qg, k_sel.astype(f32)) * (HEAD_DIM ** -0.5)
    logits = jnp.where(valid[:, :, None, None, :], logits, NEG)
    p = jax.nn.softmax(logits, axis=-1)
    o = jnp.einsum('btkgs,btskd->btkgd', p, v_sel.astype(f32))
    return o.reshape(B, Tq, ATTN_W).astype(q.dtype)


def rg_lru_branch(xl, conv0, h0, conv_w, conv_b, wa, ba, wx, bx, lam):
    f32 = jnp.float32
    B, T = xl.shape[0], xl.shape[1]
    xpad = jnp.concatenate([conv0.astype(xl.dtype), xl], axis=1)
    u = conv_b
    for j in range(CONV_W):
        u = u + xpad[:, j:j + T] * conv_w[j]
    new_conv = xpad[:, -(CONV_W - 1):]
    uf = u.astype(f32)
    ub = uf.reshape(B, T, LRU_BLOCKS, LRU_BW)
    r = jax.nn.sigmoid(jnp.einsum('btnc,ncd->btnd', ub, wa.astype(f32)).reshape(B, T, LRU_W) + ba.astype(f32))
    i = jax.nn.sigmoid(jnp.einsum('btnc,ncd->btnd', ub, wx.astype(f32)).reshape(B, T, LRU_W) + bx.astype(f32))
    log_a = -LRU_C * r * jax.nn.softplus(-lam.astype(f32))
    a = jnp.exp(log_a)
    b = jnp.sqrt(-jnp.expm1(2.0 * log_a)) * (i * uf)
    b = b.at[:, 0].add(a[:, 0] * h0.astype(f32))

    def combine(left, right):
        a1, b1 = left
        a2, b2 = right
        return a1 * a2, a2 * b1 + b2

    _, h = lax.associative_scan(combine, (a, b), axis=1)
    return h, h[:, -1], new_conv


def hybrid_layer(x, pos, past, conv0, h0, blocked, topk, p):
    B, T = x.shape[0], x.shape[1]
    xn = rmsnorm(x, p['norm_mix'])
    proj = xn @ p['w_in']
    q, k, v, iq, iw, ik, xl, yl, ga, gl = jnp.split(proj, list(IN_SPLITS), axis=-1)
    q = rope_partial(rmsnorm(q.reshape(B, T, N_HEADS, HEAD_DIM), p['norm_q']), pos, ROT_DIM)
    k = rope_partial(rmsnorm(k.reshape(B, T, N_KV_HEADS, HEAD_DIM), p['norm_k']), pos, ROT_DIM)
    v = v.reshape(B, T, N_KV_HEADS, HEAD_DIM)
    iq = rope_partial(iq.reshape(B, T, IDX_HEADS, IDX_DIM), pos, IDX_ROT_DIM)
    ik = rope_partial(rmsnorm(ik, p['norm_idx_k'])[:, :, None, :], pos, IDX_ROT_DIM)[:, :, 0, :]
    iw = iw * ((IDX_HEADS * IDX_DIM) ** -0.5)
    if past is None:
        k_all, v_all, ik_all, k_pos = k, v, ik, pos
    else:
        pk, pv, pik = past
        k_all = jnp.concatenate([pk.astype(k.dtype), k], axis=1)
        v_all = jnp.concatenate([pv.astype(v.dtype), v], axis=1)
        ik_all = jnp.concatenate([pik.astype(ik.dtype), ik], axis=1)
        k_pos = jnp.concatenate([jnp.arange(pk.shape[1], dtype=jnp.int32), pos])
    if blocked:
        nb = T // Q_BLOCK

        def to_blocks(a):
            return a.reshape((B, nb, Q_BLOCK) + a.shape[2:]).swapaxes(0, 1)

        xs = (to_blocks(q), to_blocks(iq), to_blocks(iw), pos.reshape(nb, Q_BLOCK))
        o = lax.map(lambda a: sparse_attend(a[0], a[1], a[2], a[3], k_all, v_all, ik_all, k_pos, topk), xs)
        o_attn = o.swapaxes(0, 1).reshape(B, T, ATTN_W)
    else:
        o_attn = sparse_attend(q, iq, iw, pos, k_all, v_all, ik_all, k_pos, topk)
    h_seq, h_last, conv_new = rg_lru_branch(xl, conv0, h0, p['conv_w'], p['conv_b'], p['lru_wa'],
                                            p['lru_ba'], p['lru_wx'], p['lru_bx'], p['lru_lambda'])
    o_lru = h_seq.astype(x.dtype) * jax.nn.gelu(yl)
    wb = p['w_branch']
    mixed = jax.nn.sigmoid(ga) * (o_attn @ wb[:ATTN_W]) + jax.nn.sigmoid(gl) * (o_lru @ wb[ATTN_W:])
    x = x + mixed @ p['w_out']
    xn2 = rmsnorm(x, p['norm_ffn'])
    x = x + (jax.nn.silu(xn2 @ p['w_gate']) * (xn2 @ p['w_up'])) @ p['w_down']
    return x, (k, v, ik, h_last, conv_new)


def setup_inputs(seed: int = 0) -> dict:
    key = jax.random.key(seed)
    ks = jax.random.split(key, 26)
    f32 = jnp.float32
    L = DEPTH

    def nrm(k, shape, scale):
        return scale * jax.random.normal(k, shape, f32)

    a0 = jax.random.uniform(ks[18], (L, LRU_W), f32, 0.9, 0.999)
    s = a0 ** (1.0 / LRU_C)
    lam = jnp.log(s) - jnp.log1p(-s)
    return {
        'x_prompt': nrm(ks[0], (BATCH, SEQ, D_MODEL), 1.0),
        'x_sample': nrm(ks[1], (DEC_BATCH, DEC_SEQ, D_MODEL), 1.0),
        'cache_k': nrm(ks[2], (L, DEC_BATCH, PAST_LEN, N_KV_HEADS, HEAD_DIM), 1.0),
        'cache_v': nrm(ks[3], (L, DEC_BATCH, PAST_LEN, N_KV_HEADS, HEAD_DIM), 1.0),
        'cache_idx_k': nrm(ks[4], (L, DEC_BATCH, PAST_LEN, IDX_DIM), 1.0),
        'state_lru': nrm(ks[5], (L, DEC_BATCH, LRU_W), 0.5),
        'state_conv': nrm(ks[6], (L, DEC_BATCH, CONV_W - 1, LRU_W), 1.0),
        'norm_mix': 1.0 + nrm(ks[7], (L, D_MODEL), 0.02),
        'w_in': nrm(ks[8], (L, D_MODEL, IN_COLS), D_MODEL ** -0.5),
        'norm_q': 1.0 + nrm(ks[9], (L, HEAD_DIM), 0.02),
        'norm_k': 1.0 + nrm(ks[10], (L, HEAD_DIM), 0.02),
        'norm_idx_k': 1.0 + nrm(ks[11], (L, IDX_DIM), 0.02),
        'conv_w': nrm(ks[12], (L, CONV_W, LRU_W), CONV_W ** -0.5),
        'conv_b': nrm(ks[13], (L, LRU_W), 0.01),
        'lru_wa': nrm(ks[14], (L, LRU_BLOCKS, LRU_BW, LRU_BW), LRU_BW ** -0.5),
        'lru_ba': nrm(ks[15], (L, LRU_W), 0.01),
        'lru_wx': nrm(ks[16], (L, LRU_BLOCKS, LRU_BW, LRU_BW), LRU_BW ** -0.5),
        'lru_bx': nrm(ks[17], (L, LRU_W), 0.01),
        'lru_lambda': lam,
        'w_branch': nrm(ks[19], (L, ATTN_W + LRU_W, D_MODEL), D_MODEL ** -0.5),
        'w_out': nrm(ks[20], (L, D_MODEL, D_MODEL), D_MODEL ** -0.5),
        'norm_ffn': 1.0 + nrm(ks[21], (L, D_MODEL), 0.02),
        'w_gate': nrm(ks[22], (L, D_MODEL, D_FF), D_MODEL ** -0.5),
        'w_up': nrm(ks[23], (L, D_MODEL, D_FF), D_MODEL ** -0.5),
        'w_down': nrm(ks[24], (L, D_FF, D_MODEL), D_FF ** -0.5),
    }


def reference(x_prompt, x_sample, cache_k, cache_v, cache_idx_k, state_lru, state_conv,
              norm_mix, w_in, norm_q, norm_k, norm_idx_k, conv_w, conv_b, lru_wa, lru_ba,
              lru_wx, lru_bx, lru_lambda, w_branch, w_out, norm_ffn, w_gate, w_up, w_down):
    Bp, Tp = x_prompt.shape[0], x_prompt.shape[1]
    Bs, Ts = x_sample.shape[0], x_sample.shape[1]
    past_len = cache_k.shape[2]
    topk_prompt = min(TOPK_MAX, Tp // 4)
    topk_sample = min(TOPK_MAX, (past_len + Ts) // 4)
    pos_p = jnp.arange(Tp, dtype=jnp.int32)
    pos_s = past_len + jnp.arange(Ts, dtype=jnp.int32)
    yp, ys = x_prompt, x_sample
    kp_l, vp_l, ikp_l, hp_l, cp_l = [], [], [], [], []
    ks_l, vs_l, iks_l, hs_l, cs_l = [], [], [], [], []
    for l in range(DEPTH):
        p = {'norm_mix': norm_mix[l], 'w_in': w_in[l], 'norm_q': norm_q[l], 'norm_k': norm_k[l],
             'norm_idx_k': norm_idx_k[l], 'conv_w': conv_w[l], 'conv_b': conv_b[l],
             'lru_wa': lru_wa[l], 'lru_ba': lru_ba[l], 'lru_wx': lru_wx[l], 'lru_bx': lru_bx[l],
             'lru_lambda': lru_lambda[l], 'w_branch': w_branch[l], 'w_out': w_out[l],
             'norm_ffn': norm_ffn[l], 'w_gate': w_gate[l], 'w_up': w_up[l], 'w_down': w_down[l]}
        conv0_p = jnp.zeros((Bp, CONV_W - 1, LRU_W), x_prompt.dtype)
        h0_p = jnp.zeros((Bp, LRU_W), jnp.float32)
        yp, (kp, vp, ikp, hp, cp) = hybrid_layer(yp, pos_p, None, conv0_p, h0_p, True, topk_prompt, p)
        ys, (kss, vss, ikss, hss, css) = hybrid_layer(
            ys, pos_s, (cache_k[l], cache_v[l], cache_idx_k[l]), state_conv[l], state_lru[l],
            False, topk_sample, p)
        kp_l.append(kp); vp_l.append(vp); ikp_l.append(ikp); hp_l.append(hp); cp_l.append(cp)
        ks_l.append(kss); vs_l.append(vss); iks_l.append(ikss); hs_l.append(hss); cs_l.append(css)
    k_prompt = jnp.stack(kp_l)
    v_prompt = jnp.stack(vp_l)
    idx_k_prompt = jnp.stack(ikp_l)
    lru_prompt = jnp.stack(hp_l)
    conv_prompt = jnp.stack(cp_l)
    k_sample = jnp.stack(ks_l)
    v_sample = jnp.stack(vs_l)
    idx_k_sample = jnp.stack(iks_l)
    lru_sample = jnp.stack(hs_l)
    conv_sample = jnp.stack(cs_l)
    return (yp, ys, k_prompt, v_prompt, idx_k_prompt, lru_prompt, conv_prompt,
            k_sample, v_sample, idx_k_sample, lru_sample, conv_sample)
```

```python
import functools
import math

import numpy as np
import jax
import jax.numpy as jnp
from jax import lax
from jax.experimental import pallas as pl
from jax.experimental.pallas import tpu as pltpu

CHUNK = 64
TOPK_MAX = 256
ROPE_THETA = 500000.0
CONV_W = 4
LRU_C = 8.0
EPS = 1e-6
NEG = -1e30

LANES = 128
SUBLANES = 8
BF16_ROWS = 16
V7X_VMEM_LIMIT_BYTES = 56 * 1024 * 1024

F32 = jnp.float32
BF16 = jnp.bfloat16
I32 = jnp.int32
INT_MIN = -(2 ** 31)
_NT = (((1,), (1,)), ((), ()))


def _params(n_grid):
    return pltpu.CompilerParams(
        dimension_semantics=("arbitrary",) * n_grid,
        vmem_limit_bytes=V7X_VMEM_LIMIT_BYTES)


def _row_tile(m, target):
    best = None
    for t in range(BF16_ROWS, min(m, target) + 1, BF16_ROWS):
        if m % t == 0:
            best = t
    assert best is not None, (m, target)
    return best


def _col_tile(sizes, target):
    t = target
    while t >= LANES:
        if all(s % t == 0 for s in sizes):
            return t
        t -= LANES
    raise ValueError(sizes)


def _sort_key(x):
    bits = pltpu.bitcast(x, I32)
    return bits ^ ((bits >> 31) & jnp.int32(0x7FFFFFFF))


def _np_sort_key(v):
    bits = int(np.array(v, np.float32).view(np.int32))
    return bits ^ ((bits >> 31) & 0x7FFFFFFF)


def _rmsnorm_kernel(x_ref, g_ref, o_ref):
    x = x_ref[...]
    ms = jnp.mean(x * x, axis=-1, keepdims=True)
    o_ref[...] = (x * lax.rsqrt(ms + EPS) * g_ref[...]).astype(o_ref.dtype)


def _rmsnorm(x, g):
    m, d = x.shape
    tm = _row_tile(m, 256)
    return pl.pallas_call(
        _rmsnorm_kernel,
        grid=(m // tm,),
        in_specs=[pl.BlockSpec((tm, d), lambda i: (i, 0)),
                  pl.BlockSpec((1, d), lambda i: (0, 0))],
        out_specs=pl.BlockSpec((tm, d), lambda i: (i, 0)),
        out_shape=jax.ShapeDtypeStruct((m, d), BF16),
        compiler_params=_params(1),
        name="rmsnorm",
    )(x, g.reshape(1, d))


def _proj_kernel(*refs, norm, rope, scale, hd, half):
    it = iter(refs)
    a_ref, w_ref = next(it), next(it)
    g_ref = next(it) if norm else None
    c_ref, s_ref = (next(it), next(it)) if rope else (None, None)
    outs = list(it)
    acc = jnp.dot(a_ref[...], w_ref[...], preferred_element_type=F32)
    if scale is not None:
        acc = acc * scale
    if not (norm or rope):
        for o in outs:
            o[...] = acc.astype(o.dtype)
        return
    tm, tn = acc.shape
    if rope:
        first = lax.broadcasted_iota(I32, (tm, hd), 1) < half
    for h in range(tn // hd):
        xh = acc[:, h * hd:(h + 1) * hd]
        if norm:
            ms = jnp.mean(xh * xh, axis=-1, keepdims=True)
            xh = xh * lax.rsqrt(ms + EPS) * g_ref[...]
        if rope:
            partner = jnp.where(first, pltpu.roll(xh, hd - half, 1), pltpu.roll(xh, half, 1))
            xh = xh * c_ref[...] + partner * s_ref[...]
        for o in outs:
            o[:, h * hd:(h + 1) * hd] = xh.astype(o.dtype)


def _project(a, w, col_off, n, tn, out_dtypes, *, gain=None, rope_tabs=None, scale=None, hd=LANES):
    m, k = a.shape
    tm = _row_tile(m, 1088)
    assert col_off % tn == 0 and n % tn == 0
    off = col_off // tn
    in_specs = [pl.BlockSpec((tm, k), lambda i, j: (i, 0)),
                pl.BlockSpec((k, tn), lambda i, j: (0, j + off))]
    args = [a, w]
    if gain is not None:
        in_specs.append(pl.BlockSpec((1, hd), lambda i, j: (0, 0)))
        args.append(gain.reshape(1, hd))
    if rope_tabs is not None:
        in_specs += [pl.BlockSpec((tm, hd), lambda i, j: (i, 0))] * 2
        args += list(rope_tabs)
    kern = functools.partial(_proj_kernel, norm=gain is not None, rope=rope_tabs is not None,
                             scale=scale, hd=hd, half=hd // 8)
    outs = pl.pallas_call(
        kern,
        grid=(m // tm, n // tn),
        in_specs=in_specs,
        out_specs=[pl.BlockSpec((tm, tn), lambda i, j: (i, j)) for _ in out_dtypes],
        out_shape=[jax.ShapeDtypeStruct((m, n), dt) for dt in out_dtypes],
        compiler_params=_params(2),
        name="in_proj",
    )(*args)
    return outs


def _select_to_bias(key_ref, bias_ref, m_ref, nblk, tks, topk, idx_bits, key_valid_min):
    tq = key_ref.shape[0]
    static = isinstance(nblk, int)
    kf = float(topk)

    def loop(body, init):
        if static:
            c = init
            for j in range(nblk):
                c = body(j, c)
            return c
        return lax.fori_loop(0, nblk, body, init)

    def blk(j):
        return pl.ds(j * tks if static else pl.multiple_of(j * tks, tks), tks)

    def idx(j):
        return j * tks + lax.broadcasted_iota(I32, (tq, tks), 1)

    def fold(x):
        p = x[:, :LANES]
        for c in range(1, tks // LANES):
            p = p + x[:, c * LANES:(c + 1) * LANES]
        return p

    def count(pred):
        def body(j, c):
            return c + fold(jnp.where(pred(key_ref[:, blk(j)], j), 1.0, 0.0))
        return jnp.sum(loop(body, jnp.zeros((tq, LANES), F32)), axis=1, keepdims=True)

    def value_bit(it, thr):
        cand = thr + (jnp.int32(1) << (31 - it))
        return jnp.where(count(lambda kb, j: kb >= cand) >= kf, cand, thr)

    thr = lax.fori_loop(0, 32, value_bit, jnp.full((tq, 1), INT_MIN, I32))
    need = kf - count(lambda kb, j: kb > thr)
    n_ge = count(lambda kb, j: kb >= thr)

    m_ref[...] = jnp.full(m_ref.shape, 2 ** idx_bits, I32)

    @pl.when(jnp.max(n_ge) > kf)
    def _():
        def index_bit(it, lim):
            cand = lim + (jnp.int32(1) << (idx_bits - 1 - it))
            n = count(lambda kb, j: jnp.where(kb == thr, idx(j), 2 ** idx_bits) < cand)
            return jnp.where(n <= need, cand, lim)
        lim = lax.fori_loop(0, idx_bits, index_bit, jnp.zeros((tq, 1), I32))
        m_ref[...] = jnp.broadcast_to(lim, m_ref.shape)

    lim = m_ref[:, :1]

    def write(j, c):
        kb = key_ref[:, blk(j)]
        ok = jnp.where(kb > key_valid_min, 0.0, NEG)
        tie = jnp.where(idx(j) < lim, ok, NEG)
        bias_ref[:, blk(j)] = jnp.where(kb > thr, ok, jnp.where(kb == thr, tie, NEG))
        return c
    loop(write, 0)


def _indexer_scores(iq_ref, iw, ikb, nh, dk):
    acc = None
    for h in range(nh):
        d = lax.dot_general(iq_ref[:, h * dk:(h + 1) * dk], ikb, _NT, preferred_element_type=F32)
        term = jnp.maximum(d, 0.0) * iw[:, h:h + 1]
        acc = term if acc is None else acc + term
    return acc


def _select_prompt_kernel(iq_ref, iw_ref, ik_ref, bias_ref, key_ref, m_ref, *,
                          tq, nh, dk, topk, nblk_total, idx_bits, key_valid_min, chunk_shift):
    qi = pl.program_id(1)
    nkv = qi + 1
    iw = iw_ref[...]

    def score_block(j, c):
        off = pl.multiple_of(j * tq, tq)
        acc = _indexer_scores(iq_ref, iw, ik_ref[pl.ds(off, tq), :], nh, dk)
        qpos = qi * tq + lax.broadcasted_iota(I32, (tq, tq), 0)
        kpos = off + lax.broadcasted_iota(I32, (tq, tq), 1)
        allowed = (kpos >> chunk_shift) <= (qpos >> chunk_shift)
        key_ref[:, pl.ds(off, tq)] = _sort_key(jnp.where(allowed, acc, NEG))
        return c
    lax.fori_loop(0, nkv, score_block, 0)

    _select_to_bias(key_ref, bias_ref, m_ref, nkv, tq, topk, idx_bits, key_valid_min)

    def fill(j, c):
        bias_ref[:, pl.ds(pl.multiple_of(j * tq, tq), tq)] = jnp.full((tq, tq), NEG, F32)
        return c
    lax.fori_loop(nkv, nblk_total, fill, 0)


def _select_prompt(iq, iw, ik, nb, t, tq, topk, chunk_shift):
    nh = iw.shape[1]
    dk = ik.shape[1]
    nq = t // tq
    kern = functools.partial(
        _select_prompt_kernel, tq=tq, nh=nh, dk=dk, topk=topk, nblk_total=nq,
        idx_bits=int(t).bit_length(), key_valid_min=_np_sort_key(0.5 * NEG), chunk_shift=chunk_shift)
    return pl.pallas_call(
        kern,
        grid=(nb, nq),
        in_specs=[pl.BlockSpec((tq, nh * dk), lambda b, i: (b * nq + i, 0)),
                  pl.BlockSpec((tq, nh), lambda b, i: (b * nq + i, 0)),
                  pl.BlockSpec((t, dk), lambda b, i: (b, 0))],
        out_specs=pl.BlockSpec((tq, t), lambda b, i: (b * nq + i, 0)),
        out_shape=jax.ShapeDtypeStruct((nb * t, t), F32),
        scratch_shapes=[pltpu.VMEM((tq, t), I32), pltpu.VMEM((tq, LANES), I32)],
        compiler_params=_params(2),
        name="select_prompt",
    )(iq, iw, ik)


def _select_sample_kernel(iq_ref, iw_ref, ikc_ref, ikn_ref, bias_ref, key_ref, m_ref, *,
                          ts, past, sw, nh, dk, topk, idx_bits, key_valid_min, chunk_shift):
    iw = iw_ref[...]
    for j in range(past // sw):
        ikb = ikc_ref[0, j * sw:(j + 1) * sw, :].astype(BF16)
        acc = _indexer_scores(iq_ref, iw, ikb, nh, dk)
        qpos = past + lax.broadcasted_iota(I32, (ts, sw), 0)
        kpos = j * sw + lax.broadcasted_iota(I32, (ts, sw), 1)
        allowed = (kpos >> chunk_shift) <= (qpos >> chunk_shift)
        key_ref[:, j * sw:(j + 1) * sw] = _sort_key(jnp.where(allowed, acc, NEG))
    key_ref[:, past:past + LANES] = jnp.full((ts, LANES), _np_sort_key(NEG), I32)
    acc = _indexer_scores(iq_ref, iw, ikn_ref[...], nh, dk)
    qpos = past + lax.broadcasted_iota(I32, (ts, ts), 0)
    kpos = past + lax.broadcasted_iota(I32, (ts, ts), 1)
    allowed = (kpos >> chunk_shift) <= (qpos >> chunk_shift)
    key_ref[:, past:past + ts] = _sort_key(jnp.where(allowed, acc, NEG))
    _select_to_bias(key_ref, bias_ref, m_ref, past // LANES + 1, LANES, topk, idx_bits, key_valid_min)


def _select_sample(iq, iw, ik_cache, ik_new, row_blk_off, nb, ts, topk, chunk_shift):
    nh = iw.shape[1]
    past, dk = ik_cache.shape[1], ik_cache.shape[2]
    assert past % LANES == 0 and ts <= LANES
    s_pad = past + LANES
    kern = functools.partial(
        _select_sample_kernel, ts=ts, past=past, sw=_col_tile((past,), 512), nh=nh, dk=dk, topk=topk,
        idx_bits=int(s_pad).bit_length(), key_valid_min=_np_sort_key(0.5 * NEG), chunk_shift=chunk_shift)
    return pl.pallas_call(
        kern,
        grid=(nb,),
        in_specs=[pl.BlockSpec((ts, nh * dk), lambda b: (row_blk_off + b, 0)),
                  pl.BlockSpec((ts, nh), lambda b: (row_blk_off + b, 0)),
                  pl.BlockSpec((1, past, dk), lambda b: (b, 0, 0)),
                  pl.BlockSpec((ts, dk), lambda b: (row_blk_off + b, 0))],
        out_specs=pl.BlockSpec((ts, s_pad), lambda b: (b, 0)),
        out_shape=jax.ShapeDtypeStruct((nb * ts, s_pad), F32),
        scratch_shapes=[pltpu.VMEM((ts, s_pad), I32), pltpu.VMEM((ts, LANES), I32)],
        compiler_params=_params(1),
        name="select_sample",
    )(iq, iw, ik_cache, ik_new)


def _attn_prompt_kernel(q_ref, k_ref, v_ref, bias_ref, o_ref, *, tq, grp, hd, scale):
    nkv = pl.program_id(1) + 1
    for hh in range(grp):
        qh = q_ref[:, hh * hd:(hh + 1) * hd]

        def body(j, carry, qh=qh):
            m, l, acc = carry
            off = pl.multiple_of(j * tq, tq)
            s = lax.dot_general(qh, k_ref[pl.ds(off, tq), :], _NT, preferred_element_type=F32)
            s = s * scale + bias_ref[:, pl.ds(off, tq)]
            m_new = jnp.maximum(m, jnp.max(s, axis=-1, keepdims=True))
            alpha = jnp.exp(m - m_new)
            p = jnp.exp(s - m_new)
            l = alpha * l + jnp.sum(p, axis=-1, keepdims=True)
            acc = alpha * acc + jnp.dot(p.astype(BF16), v_ref[pl.ds(off, tq), :],
                                        preferred_element_type=F32)
            return m_new, l, acc

        init = (jnp.full((tq, 1), -jnp.inf, F32), jnp.zeros((tq, 1), F32), jnp.zeros((tq, hd), F32))
        _, l, acc = lax.fori_loop(0, nkv, body, init)
        o_ref[:, hh * hd:(hh + 1) * hd] = (acc / l).astype(o_ref.dtype)


def _attn_prompt(q, k, v, bias, nb, t, tq, kvh, hd):
    m_rows, aw = q.shape
    grp = aw // (kvh * hd)
    nq = t // tq
    kern = functools.partial(_attn_prompt_kernel, tq=tq, grp=grp, hd=hd, scale=hd ** -0.5)
    return pl.pallas_call(
        kern,
        grid=(nb, nq, kvh),
        in_specs=[pl.BlockSpec((tq, grp * hd), lambda b, i, g: (b * nq + i, g)),
                  pl.BlockSpec((t, hd), lambda b, i, g: (b, g)),
                  pl.BlockSpec((t, hd), lambda b, i, g: (b, g)),
                  pl.BlockSpec((tq, t), lambda b, i, g: (b * nq + i, 0))],
        out_specs=pl.BlockSpec((tq, grp * hd), lambda b, i, g: (b * nq + i, g)),
        out_shape=jax.ShapeDtypeStruct((nb * t, aw), BF16),
        compiler_params=_params(3),
        name="attn_prompt",
    )(q, k, v, bias)


def _attn_sample_kernel(q_ref, kc_ref, vc_ref, kn_ref, vn_ref, bias_ref, o_ref, *,
                        ts, past, grp, hd, scale):
    kc = kc_ref[0].astype(BF16)
    vc = vc_ref[0].astype(BF16)
    kn = kn_ref[...]
    vn = vn_ref[...]
    bias_c = bias_ref[:, :past]
    bias_n = bias_ref[:, past:past + ts]
    for hh in range(grp):
        qh = q_ref[:, hh * hd:(hh + 1) * hd]
        sc = lax.dot_general(qh, kc, _NT, preferred_element_type=F32) * scale + bias_c
        sn = lax.dot_general(qh, kn, _NT, preferred_element_type=F32) * scale + bias_n
        m = jnp.maximum(jnp.max(sc, axis=-1, keepdims=True), jnp.max(sn, axis=-1, keepdims=True))
        pc = jnp.exp(sc - m)
        pn = jnp.exp(sn - m)
        l = jnp.sum(pc, axis=-1, keepdims=True) + jnp.sum(pn, axis=-1, keepdims=True)
        acc = (jnp.dot(pc.astype(BF16), vc, preferred_element_type=F32)
               + jnp.dot(pn.astype(BF16), vn, preferred_element_type=F32))
        o_ref[:, hh * hd:(hh + 1) * hd] = (acc / l).astype(o_ref.dtype)


def _attn_sample(q, k_cache, v_cache, k_new, v_new, bias, row_blk_off, nb, ts, kvh, hd):
    aw = q.shape[1]
    grp = aw // (kvh * hd)
    past = k_cache.shape[1]
    kern = functools.partial(_attn_sample_kernel, ts=ts, past=past, grp=grp, hd=hd, scale=hd ** -0.5)
    return pl.pallas_call(
        kern,
        grid=(nb, kvh),
        in_specs=[pl.BlockSpec((ts, grp * hd), lambda b, g: (row_blk_off + b, g)),
                  pl.BlockSpec((1, past, hd), lambda b, g: (b, 0, g)),
                  pl.BlockSpec((1, past, hd), lambda b, g: (b, 0, g)),
                  pl.BlockSpec((ts, hd), lambda b, g: (row_blk_off + b, g)),
                  pl.BlockSpec((ts, hd), lambda b, g: (row_blk_off + b, g)),
                  pl.BlockSpec((ts, past + LANES), lambda b, g: (b, 0))],
        out_specs=pl.BlockSpec((ts, grp * hd), lambda b, g: (b, g)),
        out_shape=jax.ShapeDtypeStruct((nb * ts, aw), BF16),
        compiler_params=_params(2),
        name="attn_sample",
    )(q, k_cache, v_cache, k_new, v_new, bias)


def _lru_kernel(xl_ref, yl_ref, c0_ref, h0_ref, cw_ref, cb_ref, wa_ref, ba_ref, wx_ref, bx_ref,
                lam_ref, o_ref, hl_ref, cn_ref, xbuf, hc, a_s, b_s, h_s, *, tt, cw, bw):
    ti = pl.program_id(2)
    pad = SUBLANES
    tail = CONV_W - 1

    @pl.when(ti == 0)
    def _():
        xbuf[pad - tail:pad, :] = c0_ref[0]
        hc[...] = jnp.broadcast_to(h0_ref[0], hc.shape)
        a_s[:pad, :] = jnp.ones((pad, cw), F32)
        b_s[:pad, :] = jnp.zeros((pad, cw), F32)

    x = xl_ref[...]
    xbuf[pad:, :] = x
    u = cb_ref[...]
    for j in range(CONV_W):
        u = u + xbuf[pad - tail + j:pad - tail + j + tt, :] * cw_ref[j:j + 1, :]
    xbuf[pad - tail:pad, :] = xbuf[pad + tt - tail:pad + tt, :]

    @pl.when(ti == pl.num_programs(2) - 1)
    def _():
        cn_ref[0] = x[tt - tail:, :]

    ub = u.astype(BF16)
    for n in range(cw // bw):
        cs = slice(n * bw, (n + 1) * bw)
        un = ub[:, cs]
        r = jax.nn.sigmoid(jnp.dot(un, wa_ref[n], preferred_element_type=F32) + ba_ref[:, cs])
        i = jax.nn.sigmoid(jnp.dot(un, wx_ref[n], preferred_element_type=F32) + bx_ref[:, cs])
        log_a = -LRU_C * r * jax.nn.softplus(-lam_ref[:, cs])
        a_s[pad:, cs] = jnp.exp(log_a)
        t = jnp.tanh(log_a)
        b_s[pad:, cs] = jnp.sqrt(-2.0 * t / (1.0 - t)) * (i * u[:, cs])

    row = lax.broadcasted_iota(I32, (tt, cw), 0) & (SUBLANES - 1)
    a = a_s[pad:, :]
    b = b_s[pad:, :]
    d = 1
    while d < SUBLANES:
        keep = row >= d
        a_sh = jnp.where(keep, a_s[pad - d:pad - d + tt, :], 1.0)
        b_sh = jnp.where(keep, b_s[pad - d:pad - d + tt, :], 0.0)
        b = a * b_sh + b
        a = a * a_sh
        a_s[pad:, :] = a
        b_s[pad:, :] = b
        d *= 2

    carry = hc[...]
    for g in range(tt // SUBLANES):
        rs = slice(pad + g * SUBLANES, pad + (g + 1) * SUBLANES)
        h = a_s[rs, :] * carry + b_s[rs, :]
        h_s[g * SUBLANES:(g + 1) * SUBLANES, :] = h
        carry = jnp.broadcast_to(h[SUBLANES - 1:, :], hc.shape)
    hc[...] = carry

    @pl.when(ti == pl.num_programs(2) - 1)
    def _():
        hl_ref[0] = carry[:1, :]

    o_ref[...] = (h_s[...] * jax.nn.gelu(yl_ref[...])).astype(o_ref.dtype)


def _lru_branch(tail_proj, xl_col, yl_col, row_off, conv0, h0, p, nb, t):
    r = conv0.shape[-1]
    bw = p['lru_wa'].shape[-1]
    cw = bw * max(1, min(r // bw, 512 // bw))
    tt = t if t <= 256 else 256
    assert t % tt == 0 and r % cw == 0 and row_off % tt == 0 and tt % SUBLANES == 0
    nt = t // tt
    ro = row_off // tt
    xo, yo = xl_col // cw, yl_col // cw
    assert xl_col % cw == 0 and yl_col % cw == 0
    nblk = cw // bw
    kern = functools.partial(_lru_kernel, tt=tt, cw=cw, bw=bw)
    vec = lambda: pl.BlockSpec((1, cw), lambda b, c, i: (0, c))
    return pl.pallas_call(
        kern,
        grid=(nb, r // cw, nt),
        in_specs=[pl.BlockSpec((tt, cw), lambda b, c, i: (ro + b * nt + i, xo + c)),
                  pl.BlockSpec((tt, cw), lambda b, c, i: (ro + b * nt + i, yo + c)),
                  pl.BlockSpec((1, CONV_W - 1, cw), lambda b, c, i: (b, 0, c)),
                  pl.BlockSpec((1, 1, cw), lambda b, c, i: (b, 0, c)),
                  pl.BlockSpec((CONV_W, cw), lambda b, c, i: (0, c)),
                  vec(),
                  pl.BlockSpec((nblk, bw, bw), lambda b, c, i: (c, 0, 0)),
                  vec(),
                  pl.BlockSpec((nblk, bw, bw), lambda b, c, i: (c, 0, 0)),
                  vec(), vec()],
        out_specs=[pl.BlockSpec((tt, cw), lambda b, c, i: (b * nt + i, c)),
                   pl.BlockSpec((1, 1, cw), lambda b, c, i: (b, 0, c)),
                   pl.BlockSpec((1, CONV_W - 1, cw), lambda b, c, i: (b, 0, c))],
        out_shape=[jax.ShapeDtypeStruct((nb * t, r), BF16),
                   jax.ShapeDtypeStruct((nb, 1, r), F32),
                   jax.ShapeDtypeStruct((nb, CONV_W - 1, r), F32)],
        scratch_shapes=[pltpu.VMEM((tt + SUBLANES, cw), F32),
                        pltpu.VMEM((SUBLANES, cw), F32),
                        pltpu.VMEM((tt + SUBLANES, cw), F32),
                        pltpu.VMEM((tt + SUBLANES, cw), F32),
                        pltpu.VMEM((tt, cw), F32)],
        compiler_params=_params(3),
        name="conv_rglru",
    )(tail_proj, tail_proj, conv0, h0.reshape(nb, 1, r), p['conv_w'], p['conv_b'].reshape(1, r),
      p['wa_bf'], p['lru_ba'].reshape(1, r), p['wx_bf'], p['lru_bx'].reshape(1, r),
      p['lru_lambda'].reshape(1, r))


def _mix_kernel(oa_ref, ol_ref, w1_ref, w2_ref, ga_ref, gl_ref, o_ref):
    ya = jnp.dot(oa_ref[...], w1_ref[...], preferred_element_type=F32)
    yl = jnp.dot(ol_ref[...], w2_ref[...], preferred_element_type=F32)
    mixed = jax.nn.sigmoid(ga_ref[...]) * ya + jax.nn.sigmoid(gl_ref[...]) * yl
    o_ref[...] = mixed.astype(o_ref.dtype)


def _mix(oa, ol, w1, w2, tail_proj, ga_col, gl_col):
    m, ka = oa.shape
    kl = ol.shape[1]
    n = w1.shape[1]
    tm = _row_tile(m, 544)
    tn = _col_tile((n, ga_col, gl_col), 512)
    go, lo = ga_col // tn, gl_col // tn
    return pl.pallas_call(
        _mix_kernel,
        grid=(m // tm, n // tn),
        in_specs=[pl.BlockSpec((tm, ka), lambda i, j: (i, 0)),
                  pl.BlockSpec((tm, kl), lambda i, j: (i, 0)),
                  pl.BlockSpec((ka, tn), lambda i, j: (0, j)),
                  pl.BlockSpec((kl, tn), lambda i, j: (0, j)),
                  pl.BlockSpec((tm, tn), lambda i, j: (i, go + j)),
                  pl.BlockSpec((tm, tn), lambda i, j: (i, lo + j))],
        out_specs=pl.BlockSpec((tm, tn), lambda i, j: (i, j)),
        out_shape=jax.ShapeDtypeStruct((m, n), BF16),
        compiler_params=_params(2),
        name="branch_mix",
    )(oa, ol, w1, w2, tail_proj, tail_proj)


def _residual_mm_kernel(a_ref, w_ref, x_ref, o_ref):
    o_ref[...] = x_ref[...] + jnp.dot(a_ref[...], w_ref[...], preferred_element_type=F32)


def _residual_mm(a, w, x, tm_target, tn_target):
    m, k = a.shape
    n = w.shape[1]
    tm = _row_tile(m, tm_target)
    tn = _col_tile((n,), tn_target)
    return pl.pallas_call(
        _residual_mm_kernel,
        grid=(m // tm, n // tn),
        in_specs=[pl.BlockSpec((tm, k), lambda i, j: (i, 0)),
                  pl.BlockSpec((k, tn), lambda i, j: (0, j)),
                  pl.BlockSpec((tm, tn), lambda i, j: (i, j))],
        out_specs=pl.BlockSpec((tm, tn), lambda i, j: (i, j)),
        out_shape=jax.ShapeDtypeStruct((m, n), F32),
        compiler_params=_params(2),
        name="residual_matmul",
    )(a, w, x)


def _swiglu_kernel(a_ref, wg_ref, wu_ref, o_ref):
    a = a_ref[...]
    g = jnp.dot(a, wg_ref[...], preferred_element_type=F32)
    u = jnp.dot(a, wu_ref[...], preferred_element_type=F32)
    o_ref[...] = (jax.nn.silu(g) * u).astype(o_ref.dtype)


def _swiglu(a, wg, wu):
    m, k = a.shape
    n = wg.shape[1]
    tm = _row_tile(m, 1088)
    tn = _col_tile((n,), 256)
    return pl.pallas_call(
        _swiglu_kernel,
        grid=(m // tm, n // tn),
        in_specs=[pl.BlockSpec((tm, k), lambda i, j: (i, 0)),
                  pl.BlockSpec((k, tn), lambda i, j: (0, j)),
                  pl.BlockSpec((k, tn), lambda i, j: (0, j))],
        out_specs=pl.BlockSpec((tm, tn), lambda i, j: (i, j)),
        out_shape=jax.ShapeDtypeStruct((m, n), BF16),
        compiler_params=_params(2),
        name="swiglu",
    )(a, wg, wu)


def _rope_tables(pos, hd):
    rot = hd // 4
    half = rot // 2
    inv = ROPE_THETA ** (-jnp.arange(half, dtype=F32) * 2.0 / rot)
    ang = pos.astype(F32)[:, None] * inv[None, :]
    cos, sin = jnp.cos(ang), jnp.sin(ang)
    n = pos.shape[0]
    c = jnp.concatenate([cos, cos, jnp.ones((n, hd - rot), F32)], axis=-1)
    s = jnp.concatenate([-sin, sin, jnp.zeros((n, hd - rot), F32)], axis=-1)
    return c, s


def _layer(x, dims, caches, states, p):
    bp, tp, bs, ts, past, kvh, hd, di, ih = dims
    cache_k, cache_v, cache_ik = caches
    state_lru, state_conv = states
    m, d = x.shape
    mp = bp * tp
    r = state_lru.shape[-1]
    aw = p['wb1'].shape[0]
    kvw = kvh * hd
    assert hd == LANES and di == LANES
    chunk_shift = CHUNK.bit_length() - 1
    assert 1 << chunk_shift == CHUNK

    pos = jnp.concatenate([jnp.tile(jnp.arange(tp, dtype=I32), bp),
                           jnp.tile(past + jnp.arange(ts, dtype=I32), bs)])
    tabs = _rope_tables(pos, hd)

    xn = _rmsnorm(x, p['norm_mix'])
    tn = _col_tile((aw, kvw, ih * di), 512)
    c_k, c_v, c_iq = aw, aw + kvw, aw + 2 * kvw
    (q_bf,) = _project(xn, p['w_head'], 0, aw, tn, (BF16,), gain=p['norm_q'], rope_tabs=tabs)
    k_f, k_bf = _project(xn, p['w_head'], c_k, kvw, tn, (F32, BF16), gain=p['norm_k'], rope_tabs=tabs)
    v_f, v_bf = _project(xn, p['w_head'], c_v, kvw, tn, (F32, BF16))
    (iq_bf,) = _project(xn, p['w_head'], c_iq, ih * di, tn, (BF16,), rope_tabs=tabs)
    (iw,) = _project(xn, p['w_iw'], 0, ih, ih, (F32,), scale=float((ih * di) ** -0.5))
    ik_f, ik_bf = _project(xn, p['w_ik'], 0, di, di, (F32, BF16), gain=p['norm_idx_k'], rope_tabs=tabs)
    tn_tail = _col_tile((r, d), 512)
    (tail,) = _project(xn, p['w_tail'], 0, 2 * r + 2 * d, tn_tail, (F32,))
    xl_col, yl_col, ga_col, gl_col = 0, r, 2 * r, 2 * r + d

    tq = 256 if tp % 256 == 0 else LANES
    bias_p = _select_prompt(iq_bf, iw, ik_bf, bp, tp, tq, min(TOPK_MAX, tp // 4), chunk_shift)
    oa_p = _attn_prompt(q_bf, k_bf, v_bf, bias_p, bp, tp, tq, kvh, hd)
    assert mp % ts == 0
    sblk = mp // ts
    bias_s = _select_sample(iq_bf, iw, cache_ik, ik_bf, sblk, bs, ts,
                            min(TOPK_MAX, (past + ts) // 4), chunk_shift)
    oa_s = _attn_sample(q_bf, cache_k.reshape(bs, past, kvw), cache_v.reshape(bs, past, kvw),
                        k_bf, v_bf, bias_s, sblk, bs, ts, kvh, hd)
    o_attn = jnp.concatenate([oa_p, oa_s], axis=0)

    ol_p, h_p, c_p = _lru_branch(tail, xl_col, yl_col, 0, jnp.zeros((bp, CONV_W - 1, r), F32),
                                 jnp.zeros((bp, r), F32), p, bp, tp)
    ol_s, h_s, c_s = _lru_branch(tail, xl_col, yl_col, mp, state_conv, state_lru, p, bs, ts)
    o_lru = jnp.concatenate([ol_p, ol_s], axis=0)

    mixed = _mix(o_attn, o_lru, p['wb1'], p['wb2'], tail, ga_col, gl_col)
    x1 = _residual_mm(mixed, p['w_out'], x, 1088, 512)
    xn2 = _rmsnorm(x1, p['norm_ffn'])
    hff = _swiglu(xn2, p['w_gate'], p['w_up'])
    y = _residual_mm(hff, p['w_down'], x1, 544, 256)

    new_p = (k_f[:mp].reshape(bp, tp, kvh, hd), v_f[:mp].reshape(bp, tp, kvh, hd),
             ik_f[:mp].reshape(bp, tp, di), h_p.reshape(bp, r), c_p)
    new_s = (k_f[mp:].reshape(bs, ts, kvh, hd), v_f[mp:].reshape(bs, ts, kvh, hd),
             ik_f[mp:].reshape(bs, ts, di), h_s.reshape(bs, r), c_s)
    return y, new_p, new_s


def kernel(x_prompt, x_sample, cache_k, cache_v, cache_idx_k, state_lru, state_conv, norm_mix, w_in,
           norm_q, norm_k, norm_idx_k, conv_w, conv_b, lru_wa, lru_ba, lru_wx, lru_bx, lru_lambda,
           w_branch, w_out, norm_ffn, w_gate, w_up, w_down):
    depth = w_in.shape[0]
    bp, tp, d = x_prompt.shape
    bs, ts, _ = x_sample.shape
    past, kvh, hd = cache_k.shape[2], cache_k.shape[3], cache_k.shape[4]
    di = cache_idx_k.shape[-1]
    r = state_lru.shape[-1]
    aw = w_branch.shape[1] - r
    kvw = kvh * hd
    ih = (w_in.shape[-1] - aw - 2 * kvw - di - 2 * r - 2 * d) // (di + 1)
    c_iw = aw + 2 * kvw + ih * di
    c_ik = c_iw + ih
    c_xl = c_ik + di
    dims = (bp, tp, bs, ts, past, kvh, hd, di, ih)
    mp = bp * tp

    x = jnp.concatenate([x_prompt.reshape(mp, d), x_sample.reshape(bs * ts, d)], axis=0)
    news_p, news_s = [], []
    for l in range(depth):
        p = {
            'norm_mix': norm_mix[l], 'norm_q': norm_q[l], 'norm_k': norm_k[l],
            'norm_idx_k': norm_idx_k[l], 'norm_ffn': norm_ffn[l],
            'w_head': w_in[l, :, :c_iw].astype(BF16),
            'w_iw': w_in[l, :, c_iw:c_ik].astype(BF16),
            'w_ik': w_in[l, :, c_ik:c_xl].astype(BF16),
            'w_tail': w_in[l, :, c_xl:].astype(BF16),
            'conv_w': conv_w[l], 'conv_b': conv_b[l],
            'lru_wa': lru_wa[l], 'wa_bf': lru_wa[l].astype(BF16), 'wx_bf': lru_wx[l].astype(BF16),
            'lru_ba': lru_ba[l], 'lru_bx': lru_bx[l], 'lru_lambda': lru_lambda[l],
            'wb1': w_branch[l, :aw].astype(BF16), 'wb2': w_branch[l, aw:].astype(BF16),
            'w_out': w_out[l].astype(BF16), 'w_gate': w_gate[l].astype(BF16),
            'w_up': w_up[l].astype(BF16), 'w_down': w_down[l].astype(BF16),
        }
        x, new_p, new_s = _layer(x, dims, (cache_k[l], cache_v[l], cache_idx_k[l]),
                                 (state_lru[l], state_conv[l]), p)
        news_p.append(new_p)
        news_s.append(new_s)

    stack = lambda news, i: jnp.stack([n[i] for n in news])
    return (x[:mp].reshape(bp, tp, d), x[mp:].reshape(bs, ts, d),
            stack(news_p, 0), stack(news_p, 1), stack(news_p, 2), stack(news_p, 3), stack(news_p, 4),
            stack(news_s, 0), stack(news_s, 1), stack(news_s, 2), stack(news_s, 3), stack(news_s, 4))
```

```python
import functools
import math

import numpy as np
import jax
import jax.numpy as jnp
from jax import lax
from jax.experimental import pallas as pl
from jax.experimental.pallas import tpu as pltpu

CHUNK = 64
TOPK_MAX = 256
ROPE_THETA = 500000.0
CONV_W = 4
LRU_C = 8.0
EPS = 1e-6
NEG = -1e30

LANES = 128
SUBLANES = 8
BF16_ROWS = 16
V7X_VMEM_LIMIT_BYTES = 56 * 1024 * 1024

F32 = jnp.float32
BF16 = jnp.bfloat16
I32 = jnp.int32
INT_MIN = -(2 ** 31)
_NT = (((1,), (1,)), ((), ()))


def _params(n_grid):
    return pltpu.CompilerParams(
        dimension_semantics=("arbitrary",) * n_grid,
        vmem_limit_bytes=V7X_VMEM_LIMIT_BYTES)


def _row_tile(m, target):
    best = None
    for t in range(BF16_ROWS, min(m, target) + 1, BF16_ROWS):
        if m % t == 0:
            best = t
    assert best is not None, (m, target)
    return best


def _col_tile(sizes, target):
    t = target
    while t >= LANES:
        if all(s % t == 0 for s in sizes):
            return t
        t -= LANES
    raise ValueError(sizes)


def _sort_key(x):
    bits = pltpu.bitcast(x, I32)
    return bits ^ ((bits >> 31) & jnp.int32(0x7FFFFFFF))


def _np_sort_key(v):
    bits = int(np.array(v, np.float32).view(np.int32))
    return bits ^ ((bits >> 31) & 0x7FFFFFFF)


def _rmsnorm_kernel(xp_ref, xs_ref, g_ref, o_ref, *, n_prompt_blocks):
    def norm(x_ref):
        x = x_ref[...]
        ms = jnp.mean(x * x, axis=-1, keepdims=True)
        o_ref[...] = (x * lax.rsqrt(ms + EPS) * g_ref[...]).astype(o_ref.dtype)

    @pl.when(pl.program_id(0) < n_prompt_blocks)
    def _():
        norm(xp_ref)

    @pl.when(pl.program_id(0) >= n_prompt_blocks)
    def _():
        norm(xs_ref)


def _rmsnorm(xp, xs, g):
    mp, d = xp.shape
    ms = xs.shape[0]
    tm = _row_tile(math.gcd(mp, ms), 256)
    npb = mp // tm
    return pl.pallas_call(
        functools.partial(_rmsnorm_kernel, n_prompt_blocks=npb),
        grid=((mp + ms) // tm,),
        in_specs=[pl.BlockSpec((tm, d), lambda i: (jnp.minimum(i, npb - 1), 0)),
                  pl.BlockSpec((tm, d), lambda i: (jnp.maximum(i - npb, 0), 0)),
                  pl.BlockSpec((1, d), lambda i: (0, 0))],
        out_specs=pl.BlockSpec((tm, d), lambda i: (i, 0)),
        out_shape=jax.ShapeDtypeStruct((mp + ms, d), BF16),
        compiler_params=_params(1),
        name="rmsnorm",
    )(xp, xs, g.reshape(1, d))


def _proj_kernel(*refs, norm, rope, scale, hd, half):
    it = iter(refs)
    a_ref, w_ref = next(it), next(it)
    g_ref = next(it) if norm else None
    c_ref, s_ref = (next(it), next(it)) if rope else (None, None)
    outs = list(it)
    acc = jnp.dot(a_ref[...], w_ref[...], preferred_element_type=F32)
    if scale is not None:
        acc = acc * scale
    if not (norm or rope):
        for o in outs:
            o[...] = acc.astype(o.dtype)
        return
    tm, tn = acc.shape
    if rope:
        first = lax.broadcasted_iota(I32, (tm, hd), 1) < half
    for h in range(tn // hd):
        xh = acc[:, h * hd:(h + 1) * hd]
        if norm:
            ms = jnp.mean(xh * xh, axis=-1, keepdims=True)
            xh = xh * lax.rsqrt(ms + EPS) * g_ref[...]
        if rope:
            partner = jnp.where(first, pltpu.roll(xh, hd - half, 1), pltpu.roll(xh, half, 1))
            xh = xh * c_ref[...] + partner * s_ref[...]
        for o in outs:
            o[:, h * hd:(h + 1) * hd] = xh.astype(o.dtype)


def _project(a, w, col_off, n, tn, out_dtypes, *, rows=None, gain=None, rope_tabs=None, scale=None,
             hd=LANES):
    k = a.shape[1]
    row0, m = rows if rows is not None else (0, a.shape[0])
    tm = _row_tile(math.gcd(m, row0) if row0 else m, 1088)
    ro = row0 // tm
    assert col_off % tn == 0 and n % tn == 0
    off = col_off // tn
    in_specs = [pl.BlockSpec((tm, k), lambda i, j: (ro + i, 0)),
                pl.BlockSpec((k, tn), lambda i, j: (0, j + off))]
    args = [a, w]
    if gain is not None:
        in_specs.append(pl.BlockSpec((1, hd), lambda i, j: (0, 0)))
        args.append(gain.reshape(1, hd))
    if rope_tabs is not None:
        in_specs += [pl.BlockSpec((tm, hd), lambda i, j: (ro + i, 0))] * 2
        args += list(rope_tabs)
    kern = functools.partial(_proj_kernel, norm=gain is not None, rope=rope_tabs is not None,
                             scale=scale, hd=hd, half=hd // 8)
    outs = pl.pallas_call(
        kern,
        grid=(m // tm, n // tn),
        in_specs=in_specs,
        out_specs=[pl.BlockSpec((tm, tn), lambda i, j: (i, j)) for _ in out_dtypes],
        out_shape=[jax.ShapeDtypeStruct((m, n), dt) for dt in out_dtypes],
        compiler_params=_params(2),
        name="in_proj",
    )(*args)
    return outs


def _select_to_bias(key_ref, bias_ref, lim_ref, nblk, tks, topk, idx_bits, key_valid_min, axis):
    nq = key_ref.shape[1 - axis]
    static = isinstance(nblk, int)
    kf = float(topk)
    blk_shape = (nq, tks) if axis == 1 else (tks, nq)

    def loop(body, init):
        if static:
            c = init
            for j in range(nblk):
                c = body(j, c)
            return c
        return lax.fori_loop(0, nblk, body, init)

    def at(j):
        s = pl.ds(j * tks if static else pl.multiple_of(j * tks, tks), tks)
        return (slice(None), s) if axis == 1 else (s, slice(None))

    def idx(j):
        return j * tks + lax.broadcasted_iota(I32, blk_shape, axis)

    def fold(x):
        if axis == 1:
            p = x[:, :LANES]
            for c in range(1, tks // LANES):
                p = p + x[:, c * LANES:(c + 1) * LANES]
            return p
        return jnp.sum(x.reshape(tks // SUBLANES, SUBLANES, nq), axis=0)

    part_shape = (nq, LANES) if axis == 1 else (SUBLANES, nq)
    vec_shape = (nq, 1) if axis == 1 else (1, nq)

    def count(pred):
        def body(j, c):
            return c + fold(jnp.where(pred(key_ref[at(j)], j), jnp.float32(1), jnp.float32(0)))
        return jnp.sum(loop(body, jnp.zeros(part_shape, F32)), axis=axis, keepdims=True)

    def value_bit(it, thr):
        cand = thr + (jnp.int32(1) << (31 - it))
        return jnp.where(count(lambda kb, j: kb >= cand) >= kf, cand, thr)

    thr = lax.fori_loop(0, 32, value_bit, jnp.full(vec_shape, INT_MIN, I32))
    need = kf - count(lambda kb, j: kb > thr)
    n_ge = count(lambda kb, j: kb >= thr)

    lim_ref[...] = jnp.full(lim_ref.shape, 2 ** idx_bits, I32)

    @pl.when(jnp.max(n_ge) > kf)
    def _():
        def index_bit(it, lim):
            cand = lim + (jnp.int32(1) << (idx_bits - 1 - it))
            n = count(lambda kb, j: jnp.where(kb == thr, idx(j), 2 ** idx_bits) < cand)
            return jnp.where(n <= need, cand, lim)
        lim = lax.fori_loop(0, idx_bits, index_bit, jnp.zeros(vec_shape, I32))
        lim_ref[...] = jnp.broadcast_to(lim, lim_ref.shape)

    lim = lim_ref[:, :1] if axis == 1 else lim_ref[:1, :]

    def write(j, c):
        kb = key_ref[at(j)]
        ok = jnp.where(kb > key_valid_min, 0.0, NEG)
        tie = jnp.where(idx(j) < lim, ok, NEG)
        bias_ref[at(j)] = jnp.where(kb > thr, ok, jnp.where(kb == thr, tie, NEG))
        return c
    loop(write, 0)


def _indexer_scores(iq_ref, iw, ikb, nh, dk):
    acc = None
    for h in range(nh):
        d = lax.dot_general(iq_ref[:, h * dk:(h + 1) * dk], ikb, _NT, preferred_element_type=F32)
        term = jnp.maximum(d, 0.0) * iw[:, h:h + 1]
        acc = term if acc is None else acc + term
    return acc


def _select_prompt_kernel(iq_ref, iwt_ref, ik_ref, bias_ref, key_ref, lim_ref, *,
                          tq, tk, nh, dk, topk, nblk_total, idx_bits, key_valid_min, chunk_shift):
    qi = pl.program_id(1)
    nkv = ((qi + 1) * tq + tk - 1) // tk

    def score_block(j, c):
        off = pl.multiple_of(j * tk, tk)
        ikb = ik_ref[pl.ds(off, tk), :]
        acc = None
        for h in range(nh):
            d = lax.dot_general(ikb, iq_ref[:, h * dk:(h + 1) * dk], _NT, preferred_element_type=F32)
            term = jnp.maximum(d, 0.0) * iwt_ref[h:h + 1, :]
            acc = term if acc is None else acc + term
        kpos = off + lax.broadcasted_iota(I32, (tk, tq), 0)
        qpos = qi * tq + lax.broadcasted_iota(I32, (tk, tq), 1)
        allowed = (kpos >> chunk_shift) <= (qpos >> chunk_shift)
        key_ref[pl.ds(off, tk), :] = _sort_key(jnp.where(allowed, acc, NEG))
        return c
    lax.fori_loop(0, nkv, score_block, 0)

    bias = bias_ref.at[0]
    _select_to_bias(key_ref, bias, lim_ref, nkv, tk, topk, idx_bits, key_valid_min, axis=0)

    def fill(j, c):
        bias[pl.ds(pl.multiple_of(j * tk, tk), tk), :] = jnp.full((tk, tq), NEG, F32)
        return c
    lax.fori_loop(nkv, nblk_total, fill, 0)


def _select_prompt(iq, iwt, ik, nb, t, tq, tk, topk, chunk_shift):
    nh = iwt.shape[0]
    dk = ik.shape[1]
    nq = t // tq
    kern = functools.partial(
        _select_prompt_kernel, tq=tq, tk=tk, nh=nh, dk=dk, topk=topk, nblk_total=t // tk,
        idx_bits=int(t).bit_length(), key_valid_min=_np_sort_key(0.5 * NEG), chunk_shift=chunk_shift)
    return pl.pallas_call(
        kern,
        grid=(nb, nq),
        in_specs=[pl.BlockSpec((tq, nh * dk), lambda b, i: (b * nq + i, 0)),
                  pl.BlockSpec((nh, tq), lambda b, i: (0, b * nq + i)),
                  pl.BlockSpec((t, dk), lambda b, i: (b, 0))],
        out_specs=pl.BlockSpec((1, t, tq), lambda b, i: (b * nq + i, 0, 0)),
        out_shape=jax.ShapeDtypeStruct((nb * nq, t, tq), F32),
        scratch_shapes=[pltpu.VMEM((t, tq), I32), pltpu.VMEM((SUBLANES, tq), I32)],
        compiler_params=_params(2),
        name="select_prompt",
    )(iq, iwt, ik)


def _select_sample_kernel(iq_ref, iw_ref, ikc_ref, ikn_ref, bias_ref, key_ref, m_ref, *,
                          ts, past, sw, nh, dk, topk, idx_bits, key_valid_min, chunk_shift):
    iw = iw_ref[...]
    for j in range(past // sw):
        ikb = ikc_ref[0, j * sw:(j + 1) * sw, :].astype(BF16)
        acc = _indexer_scores(iq_ref, iw, ikb, nh, dk)
        qpos = past + lax.broadcasted_iota(I32, (ts, sw), 0)
        kpos = j * sw + lax.broadcasted_iota(I32, (ts, sw), 1)
        allowed = (kpos >> chunk_shift) <= (qpos >> chunk_shift)
        key_ref[:, j * sw:(j + 1) * sw] = _sort_key(jnp.where(allowed, acc, NEG))
    key_ref[:, past:past + LANES] = jnp.full((ts, LANES), _np_sort_key(NEG), I32)
    acc = _indexer_scores(iq_ref, iw, ikn_ref[...], nh, dk)
    qpos = past + lax.broadcasted_iota(I32, (ts, ts), 0)
    kpos = past + lax.broadcasted_iota(I32, (ts, ts), 1)
    allowed = (kpos >> chunk_shift) <= (qpos >> chunk_shift)
    key_ref[:, past:past + ts] = _sort_key(jnp.where(allowed, acc, NEG))
    _select_to_bias(key_ref, bias_ref, m_ref, past // LANES + 1, LANES, topk, idx_bits, key_valid_min,
                    axis=1)


def _select_sample(iq, iw, ik_cache, ik_new, row_blk_off, cache_off, nb, ts, topk, chunk_shift):
    nh = iw.shape[1]
    past, dk = ik_cache.shape[1], ik_cache.shape[2]
    assert past % LANES == 0 and ts <= LANES
    s_pad = past + LANES
    kern = functools.partial(
        _select_sample_kernel, ts=ts, past=past, sw=_col_tile((past,), 512), nh=nh, dk=dk, topk=topk,
        idx_bits=int(s_pad).bit_length(), key_valid_min=_np_sort_key(0.5 * NEG), chunk_shift=chunk_shift)
    return pl.pallas_call(
        kern,
        grid=(nb,),
        in_specs=[pl.BlockSpec((ts, nh * dk), lambda b: (row_blk_off + b, 0)),
                  pl.BlockSpec((ts, nh), lambda b: (row_blk_off + b, 0)),
                  pl.BlockSpec((1, past, dk), lambda b: (cache_off + b, 0, 0)),
                  pl.BlockSpec((ts, dk), lambda b: (b, 0))],
        out_specs=pl.BlockSpec((ts, s_pad), lambda b: (b, 0)),
        out_shape=jax.ShapeDtypeStruct((nb * ts, s_pad), F32),
        scratch_shapes=[pltpu.VMEM((ts, s_pad), I32), pltpu.VMEM((ts, LANES), I32)],
        compiler_params=_params(1),
        name="select_sample",
    )(iq, iw, ik_cache, ik_new)


def _attn_prompt_kernel(q_ref, k_ref, vt_ref, bias_ref, o_ref, m_ref, l_ref, acc_ref, s_ref, mc_ref, *,
                        tq, tk, grp, hd, scale_log2e):
    nkv = ((pl.program_id(1) + 1) * tq + tk - 1) // tk
    m_ref[...] = jnp.full(m_ref.shape, -jnp.inf, F32)
    l_ref[...] = jnp.zeros(l_ref.shape, F32)
    acc_ref[...] = jnp.zeros(acc_ref.shape, F32)

    def body(j, c):
        off = pl.multiple_of(j * tk, tk)
        kb = k_ref[pl.ds(off, tk), :]
        vtb = vt_ref[:, pl.ds(off, tk)]
        bias = bias_ref[0, pl.ds(off, tk), :]
        for hh in range(grp):
            qh = q_ref[:, hh * hd:(hh + 1) * hd]
            s = lax.dot_general(kb, qh, _NT, preferred_element_type=F32) * scale_log2e + bias
            s_ref[hh] = s
            mc_ref[hh] = jnp.max(s, axis=0, keepdims=True)
        for hh in range(grp):
            s = s_ref[hh]
            m = m_ref[hh]
            m_new = jnp.maximum(m, mc_ref[hh])
            alpha = jnp.exp2(m - m_new)
            p = jnp.exp2(s - m_new)
            l_ref[hh] = alpha * l_ref[hh] + jnp.sum(p, axis=0, keepdims=True)
            acc_ref[hh] = alpha * acc_ref[hh] + jnp.dot(vtb, p.astype(BF16),
                                                        preferred_element_type=F32)
            m_ref[hh] = m_new
        return c
    lax.fori_loop(0, nkv, body, 0)

    for hh in range(grp):
        o_ref[:, hh * hd:(hh + 1) * hd] = (acc_ref[hh] / l_ref[hh]).T.astype(o_ref.dtype)


def _attn_prompt(q, k, vt, bias_t, nb, t, tq, tk, kvh, hd):
    aw = q.shape[1]
    grp = aw // (kvh * hd)
    nq = t // tq
    kern = functools.partial(_attn_prompt_kernel, tq=tq, tk=tk, grp=grp, hd=hd,
                             scale_log2e=hd ** -0.5 * math.log2(math.e))
    return pl.pallas_call(
        kern,
        grid=(nb, nq, kvh),
        in_specs=[pl.BlockSpec((tq, grp * hd), lambda b, i, g: (b * nq + i, g)),
                  pl.BlockSpec((t, hd), lambda b, i, g: (b, g)),
                  pl.BlockSpec((hd, t), lambda b, i, g: (g, b)),
                  pl.BlockSpec((1, t, tq), lambda b, i, g: (b * nq + i, 0, 0))],
        out_specs=pl.BlockSpec((tq, grp * hd), lambda b, i, g: (b * nq + i, g)),
        out_shape=jax.ShapeDtypeStruct((nb * t, aw), BF16),
        scratch_shapes=[pltpu.VMEM((grp, 1, tq), F32),
                        pltpu.VMEM((grp, 1, tq), F32),
                        pltpu.VMEM((grp, hd, tq), F32),
                        pltpu.VMEM((grp, tk, tq), F32),
                        pltpu.VMEM((grp, 1, tq), F32)],
        compiler_params=_params(3),
        name="attn_prompt",
    )(q, k, vt, bias_t)


def _attn_sample_kernel(q_ref, kc_ref, vc_ref, kn_ref, vn_ref, bias_ref, o_ref, *,
                        ts, past, grp, hd, scale):
    kc = kc_ref[0].astype(BF16)
    vc = vc_ref[0].astype(BF16)
    kn = kn_ref[...]
    vn = vn_ref[...]
    bias_c = bias_ref[:, :past]
    bias_n = bias_ref[:, past:past + ts]
    for hh in range(grp):
        qh = q_ref[:, hh * hd:(hh + 1) * hd]
        sc = lax.dot_general(qh, kc, _NT, preferred_element_type=F32) * scale + bias_c
        sn = lax.dot_general(qh, kn, _NT, preferred_element_type=F32) * scale + bias_n
        m = jnp.maximum(jnp.max(sc, axis=-1, keepdims=True), jnp.max(sn, axis=-1, keepdims=True))
        pc = jnp.exp(sc - m)
        pn = jnp.exp(sn - m)
        l = jnp.sum(pc, axis=-1, keepdims=True) + jnp.sum(pn, axis=-1, keepdims=True)
        acc = (jnp.dot(pc.astype(BF16), vc, preferred_element_type=F32)
               + jnp.dot(pn.astype(BF16), vn, preferred_element_type=F32))
        o_ref[:, hh * hd:(hh + 1) * hd] = (acc / l).astype(o_ref.dtype)


def _attn_sample(q, k_cache, v_cache, k_new, v_new, bias, row_blk_off, cache_off, nb, ts, kvh, hd):
    aw = q.shape[1]
    grp = aw // (kvh * hd)
    past = k_cache.shape[1]
    kern = functools.partial(_attn_sample_kernel, ts=ts, past=past, grp=grp, hd=hd, scale=hd ** -0.5)
    return pl.pallas_call(
        kern,
        grid=(nb, kvh),
        in_specs=[pl.BlockSpec((ts, grp * hd), lambda b, g: (row_blk_off + b, g)),
                  pl.BlockSpec((1, past, hd), lambda b, g: (cache_off + b, 0, g)),
                  pl.BlockSpec((1, past, hd), lambda b, g: (cache_off + b, 0, g)),
                  pl.BlockSpec((ts, hd), lambda b, g: (b, g)),
                  pl.BlockSpec((ts, hd), lambda b, g: (b, g)),
                  pl.BlockSpec((ts, past + LANES), lambda b, g: (b, 0))],
        out_specs=pl.BlockSpec((ts, grp * hd), lambda b, g: (b, g)),
        out_shape=jax.ShapeDtypeStruct((nb * ts, aw), BF16),
        compiler_params=_params(2),
        name="attn_sample",
    )(q, k_cache, v_cache, k_new, v_new, bias)


def _lru_kernel(xl_ref, yl_ref, c0_ref, h0_ref, cw_ref, cb_ref, wa_ref, ba_ref, wx_ref, bx_ref,
                lam_ref, o_ref, hl_ref, cn_ref, xbuf, hc, a_s, b_s, h_s, *, tt, cw, bw):
    ti = pl.program_id(2)
    pad = SUBLANES
    tail = CONV_W - 1

    @pl.when(ti == 0)
    def _():
        xbuf[pad - tail:pad, :] = c0_ref[0]
        hc[...] = jnp.broadcast_to(h0_ref[0], hc.shape)
        a_s[:pad, :] = jnp.ones((pad, cw), F32)
        b_s[:pad, :] = jnp.zeros((pad, cw), F32)

    x = xl_ref[...]
    xbuf[pad:, :] = x
    u = cb_ref[...]
    for j in range(CONV_W):
        u = u + xbuf[pad - tail + j:pad - tail + j + tt, :] * cw_ref[j:j + 1, :]
    xbuf[pad - tail:pad, :] = xbuf[pad + tt - tail:pad + tt, :]

    @pl.when(ti == pl.num_programs(2) - 1)
    def _():
        cn_ref[0] = x[tt - tail:, :]

    ub = u.astype(BF16)
    for n in range(cw // bw):
        cs = slice(n * bw, (n + 1) * bw)
        un = ub[:, cs]
        r = jax.nn.sigmoid(jnp.dot(un, wa_ref[n], preferred_element_type=F32) + ba_ref[:, cs])
        i = jax.nn.sigmoid(jnp.dot(un, wx_ref[n], preferred_element_type=F32) + bx_ref[:, cs])
        log_a = -LRU_C * r * jax.nn.softplus(-lam_ref[:, cs])
        a_s[pad:, cs] = jnp.exp(log_a)
        t = jnp.tanh(log_a)
        b_s[pad:, cs] = jnp.sqrt(-2.0 * t / (1.0 - t)) * (i * u[:, cs])

    row = lax.broadcasted_iota(I32, (tt, cw), 0) & (SUBLANES - 1)
    a = a_s[pad:, :]
    b = b_s[pad:, :]
    d = 1
    while d < SUBLANES:
        keep = row >= d
        a_sh = jnp.where(keep, a_s[pad - d:pad - d + tt, :], 1.0)
        b_sh = jnp.where(keep, b_s[pad - d:pad - d + tt, :], 0.0)
        b = a * b_sh + b
        a = a * a_sh
        a_s[pad:, :] = a
        b_s[pad:, :] = b
        d *= 2

    carry = hc[...]
    for g in range(tt // SUBLANES):
        rs = slice(pad + g * SUBLANES, pad + (g + 1) * SUBLANES)
        h = a_s[rs, :] * carry + b_s[rs, :]
        h_s[g * SUBLANES:(g + 1) * SUBLANES, :] = h
        carry = jnp.broadcast_to(h[SUBLANES - 1:, :], hc.shape)
    hc[...] = carry

    @pl.when(ti == pl.num_programs(2) - 1)
    def _():
        hl_ref[0] = carry[:1, :]

    o_ref[...] = (h_s[...] * jax.nn.gelu(yl_ref[...])).astype(o_ref.dtype)


def _lru_branch(tail_proj, xl_col, yl_col, row_off, conv0, h0, p, nb, t):
    r = conv0.shape[-1]
    bw = p['lru_wa'].shape[-1]
    cw = bw * max(1, min(r // bw, 512 // bw))
    tt = t if t <= 256 else 256
    assert t % tt == 0 and r % cw == 0 and row_off % tt == 0 and tt % SUBLANES == 0
    nt = t // tt
    ro = row_off // tt
    xo, yo = xl_col // cw, yl_col // cw
    assert xl_col % cw == 0 and yl_col % cw == 0
    nblk = cw // bw
    kern = functools.partial(_lru_kernel, tt=tt, cw=cw, bw=bw)
    vec = lambda: pl.BlockSpec((1, cw), lambda b, c, i: (0, c))
    return pl.pallas_call(
        kern,
        grid=(nb, r // cw, nt),
        in_specs=[pl.BlockSpec((tt, cw), lambda b, c, i: (ro + b * nt + i, xo + c)),
                  pl.BlockSpec((tt, cw), lambda b, c, i: (ro + b * nt + i, yo + c)),
                  pl.BlockSpec((1, CONV_W - 1, cw), lambda b, c, i: (b, 0, c)),
                  pl.BlockSpec((1, 1, cw), lambda b, c, i: (b, 0, c)),
                  pl.BlockSpec((CONV_W, cw), lambda b, c, i: (0, c)),
                  vec(),
                  pl.BlockSpec((nblk, bw, bw), lambda b, c, i: (c, 0, 0)),
                  vec(),
                  pl.BlockSpec((nblk, bw, bw), lambda b, c, i: (c, 0, 0)),
                  vec(), vec()],
        out_specs=[pl.BlockSpec((tt, cw), lambda b, c, i: (b * nt + i, c)),
                   pl.BlockSpec((1, 1, cw), lambda b, c, i: (b, 0, c)),
                   pl.BlockSpec((1, CONV_W - 1, cw), lambda b, c, i: (b, 0, c))],
        out_shape=[jax.ShapeDtypeStruct((nb * t, r), BF16),
                   jax.ShapeDtypeStruct((nb, 1, r), F32),
                   jax.ShapeDtypeStruct((nb, CONV_W - 1, r), F32)],
        scratch_shapes=[pltpu.VMEM((tt + SUBLANES, cw), F32),
                        pltpu.VMEM((SUBLANES, cw), F32),
                        pltpu.VMEM((tt + SUBLANES, cw), F32),
                        pltpu.VMEM((tt + SUBLANES, cw), F32),
                        pltpu.VMEM((tt, cw), F32)],
        compiler_params=_params(3),
        name="conv_rglru",
    )(tail_proj, tail_proj, conv0, h0.reshape(nb, 1, r), p['conv_w'], p['conv_b'].reshape(1, r),
      p['wa_bf'], p['lru_ba'].reshape(1, r), p['wx_bf'], p['lru_bx'].reshape(1, r),
      p['lru_lambda'].reshape(1, r))


def _mix_kernel(oa_ref, ol_ref, w1_ref, w2_ref, ga_ref, gl_ref, o_ref):
    ya = jnp.dot(oa_ref[...], w1_ref[...], preferred_element_type=F32)
    yl = jnp.dot(ol_ref[...], w2_ref[...], preferred_element_type=F32)
    mixed = jax.nn.sigmoid(ga_ref[...]) * ya + jax.nn.sigmoid(gl_ref[...]) * yl
    o_ref[...] = mixed.astype(o_ref.dtype)


def _mix(oa, ol, w1, w2, tail_proj, row0, ga_col, gl_col):
    m, ka = oa.shape
    kl = ol.shape[1]
    n = w1.shape[1]
    tm = _row_tile(math.gcd(m, row0) if row0 else m, 512)
    ro = row0 // tm
    tn = _col_tile((n, ga_col, gl_col), 512)
    go, lo = ga_col // tn, gl_col // tn
    return pl.pallas_call(
        _mix_kernel,
        grid=(m // tm, n // tn),
        in_specs=[pl.BlockSpec((tm, ka), lambda i, j: (i, 0)),
                  pl.BlockSpec((tm, kl), lambda i, j: (i, 0)),
                  pl.BlockSpec((ka, tn), lambda i, j: (0, j)),
                  pl.BlockSpec((kl, tn), lambda i, j: (0, j)),
                  pl.BlockSpec((tm, tn), lambda i, j: (ro + i, go + j)),
                  pl.BlockSpec((tm, tn), lambda i, j: (ro + i, lo + j))],
        out_specs=pl.BlockSpec((tm, tn), lambda i, j: (i, j)),
        out_shape=jax.ShapeDtypeStruct((m, n), BF16),
        compiler_params=_params(2),
        name="branch_mix",
    )(oa, ol, w1, w2, tail_proj, tail_proj)


def _residual_mm_kernel(a_ref, w_ref, x_ref, o_ref):
    o_ref[...] = x_ref[...] + jnp.dot(a_ref[...], w_ref[...], preferred_element_type=F32)


def _residual_mm(a, row0, w, x, tm_target, tn_target):
    k = a.shape[1]
    m = x.shape[0]
    n = w.shape[1]
    tm = _row_tile(math.gcd(m, row0) if row0 else m, tm_target)
    ro = row0 // tm
    tn = _col_tile((n,), tn_target)
    return pl.pallas_call(
        _residual_mm_kernel,
        grid=(m // tm, n // tn),
        in_specs=[pl.BlockSpec((tm, k), lambda i, j: (ro + i, 0)),
                  pl.BlockSpec((k, tn), lambda i, j: (0, j)),
                  pl.BlockSpec((tm, tn), lambda i, j: (i, j))],
        out_specs=pl.BlockSpec((tm, tn), lambda i, j: (i, j)),
        out_shape=jax.ShapeDtypeStruct((m, n), F32),
        compiler_params=_params(2),
        name="residual_matmul",
    )(a, w, x)


def _swiglu_kernel(a_ref, wg_ref, wu_ref, o_ref):
    a = a_ref[...]
    g = jnp.dot(a, wg_ref[...], preferred_element_type=F32)
    u = jnp.dot(a, wu_ref[...], preferred_element_type=F32)
    o_ref[...] = (jax.nn.silu(g) * u).astype(o_ref.dtype)


def _swiglu(a, wg, wu):
    m, k = a.shape
    n = wg.shape[1]
    tm = _row_tile(m, 1088)
    tn = _col_tile((n,), 256)
    return pl.pallas_call(
        _swiglu_kernel,
        grid=(m // tm, n // tn),
        in_specs=[pl.BlockSpec((tm, k), lambda i, j: (i, 0)),
                  pl.BlockSpec((k, tn), lambda i, j: (0, j)),
                  pl.BlockSpec((k, tn), lambda i, j: (0, j))],
        out_specs=pl.BlockSpec((tm, tn), lambda i, j: (i, j)),
        out_shape=jax.ShapeDtypeStruct((m, n), BF16),
        compiler_params=_params(2),
        name="swiglu",
    )(a, wg, wu)


def _rope_tables(pos, hd):
    rot = hd // 4
    half = rot // 2
    inv = ROPE_THETA ** (-jnp.arange(half, dtype=F32) * 2.0 / rot)
    ang = pos.astype(F32)[:, None] * inv[None, :]
    cos, sin = jnp.cos(ang), jnp.sin(ang)
    n = pos.shape[0]
    c = jnp.concatenate([cos, cos, jnp.ones((n, hd - rot), F32)], axis=-1)
    s = jnp.concatenate([-sin, sin, jnp.zeros((n, hd - rot), F32)], axis=-1)
    return c, s


def _layer(xp, xs, dims, layer, caches, states, p):
    bp, tp, bs, ts, past, kvh, hd, di, ih = dims
    cache_k, cache_v, cache_ik = caches
    state_lru, state_conv = states
    d = xp.shape[1]
    mp, ms = bp * tp, bs * ts
    r = state_lru.shape[-1]
    aw = p['wb1'].shape[0]
    kvw = kvh * hd
    assert hd == LANES and di == LANES and mp % ts == 0
    chunk_shift = CHUNK.bit_length() - 1
    assert 1 << chunk_shift == CHUNK
    rows_p, rows_s = (0, mp), (mp, ms)

    pos = jnp.concatenate([jnp.tile(jnp.arange(tp, dtype=I32), bp),
                           jnp.tile(past + jnp.arange(ts, dtype=I32), bs)])
    tabs = _rope_tables(pos, hd)

    xn = _rmsnorm(xp, xs, p['norm_mix'])
    tn = _col_tile((aw, kvw, ih * di), 512)
    c_k, c_v, c_iq = aw, aw + kvw, aw + 2 * kvw
    (q_bf,) = _project(xn, p['w_head'], 0, aw, tn, (BF16,), gain=p['norm_q'], rope_tabs=tabs)
    (iq_bf,) = _project(xn, p['w_head'], c_iq, ih * di, tn, (BF16,), rope_tabs=tabs)
    (iw,) = _project(xn, p['w_iw'], 0, ih, ih, (F32,), scale=float((ih * di) ** -0.5))
    tn_tail = _col_tile((r, d), 512)
    (tail,) = _project(xn, p['w_tail'], 0, 2 * r + 2 * d, tn_tail, (F32,))
    xl_col, yl_col, ga_col, gl_col = 0, r, 2 * r, 2 * r + d
    kvi = []
    for rows in (rows_p, rows_s):
        k_f, k_bf = _project(xn, p['w_head'], c_k, kvw, tn, (F32, BF16), rows=rows,
                             gain=p['norm_k'], rope_tabs=tabs)
        v_f, v_bf = _project(xn, p['w_head'], c_v, kvw, tn, (F32, BF16), rows=rows)
        ik_f, ik_bf = _project(xn, p['w_ik'], 0, di, di, (F32, BF16), rows=rows,
                               gain=p['norm_idx_k'], rope_tabs=tabs)
        kvi.append((k_f, k_bf, v_f, v_bf, ik_f, ik_bf))
    (kp_f, kp_bf, vp_f, vp_bf, ikp_f, ikp_bf), (ks_f, ks_bf, vs_f, vs_bf, iks_f, iks_bf) = kvi

    tq = 256 if tp % 256 == 0 else LANES
    tk = 512 if tp % 512 == 0 else tq
    bias_p = _select_prompt(iq_bf, iw.T, ikp_bf, bp, tp, tq, tk, min(TOPK_MAX, tp // 4), chunk_shift)
    oa_p = _attn_prompt(q_bf, kp_bf, vp_bf.T, bias_p, bp, tp, tq, tk, kvh, hd)
    sblk = mp // ts
    bias_s = _select_sample(iq_bf, iw, cache_ik, iks_bf, sblk, layer * bs, bs, ts,
                            min(TOPK_MAX, (past + ts) // 4), chunk_shift)
    oa_s = _attn_sample(q_bf, cache_k, cache_v, ks_bf, vs_bf, bias_s, sblk, layer * bs, bs, ts, kvh, hd)

    ol_p, h_p, c_p = _lru_branch(tail, xl_col, yl_col, 0, jnp.zeros((bp, CONV_W - 1, r), F32),
                                 jnp.zeros((bp, r), F32), p, bp, tp)
    ol_s, h_s, c_s = _lru_branch(tail, xl_col, yl_col, mp, state_conv, state_lru, p, bs, ts)

    x1 = []
    for oa, ol, x, row0 in ((oa_p, ol_p, xp, 0), (oa_s, ol_s, xs, mp)):
        mixed = _mix(oa, ol, p['wb1'], p['wb2'], tail, row0, ga_col, gl_col)
        x1.append(_residual_mm(mixed, 0, p['w_out'], x, 1024, 512))
    xn2 = _rmsnorm(x1[0], x1[1], p['norm_ffn'])
    hff = _swiglu(xn2, p['w_gate'], p['w_up'])
    yp = _residual_mm(hff, 0, p['w_down'], x1[0], 512, 256)
    ys = _residual_mm(hff, mp, p['w_down'], x1[1], 512, 256)

    new_p = (kp_f.reshape(bp, tp, kvh, hd), vp_f.reshape(bp, tp, kvh, hd),
             ikp_f.reshape(bp, tp, di), h_p.reshape(bp, r), c_p)
    new_s = (ks_f.reshape(bs, ts, kvh, hd), vs_f.reshape(bs, ts, kvh, hd),
             iks_f.reshape(bs, ts, di), h_s.reshape(bs, r), c_s)
    return yp, ys, new_p, new_s


def kernel(x_prompt, x_sample, cache_k, cache_v, cache_idx_k, state_lru, state_conv, norm_mix, w_in,
           norm_q, norm_k, norm_idx_k, conv_w, conv_b, lru_wa, lru_ba, lru_wx, lru_bx, lru_lambda,
           w_branch, w_out, norm_ffn, w_gate, w_up, w_down):
    depth = w_in.shape[0]
    bp, tp, d = x_prompt.shape
    bs, ts, _ = x_sample.shape
    past, kvh, hd = cache_k.shape[2], cache_k.shape[3], cache_k.shape[4]
    di = cache_idx_k.shape[-1]
    r = state_lru.shape[-1]
    aw = w_branch.shape[1] - r
    kvw = kvh * hd
    ih = (w_in.shape[-1] - aw - 2 * kvw - di - 2 * r - 2 * d) // (di + 1)
    c_iw = aw + 2 * kvw + ih * di
    c_ik = c_iw + ih
    c_xl = c_ik + di
    dims = (bp, tp, bs, ts, past, kvh, hd, di, ih)
    xp, xs = x_prompt.reshape(bp * tp, d), x_sample.reshape(bs * ts, d)
    caches = (cache_k.reshape(depth * bs, past, kvw), cache_v.reshape(depth * bs, past, kvw),
              cache_idx_k.reshape(depth * bs, past, di))
    news_p, news_s = [], []
    for l in range(depth):
        p = {
            'norm_mix': norm_mix[l], 'norm_q': norm_q[l], 'norm_k': norm_k[l],
            'norm_idx_k': norm_idx_k[l], 'norm_ffn': norm_ffn[l],
            'w_head': w_in[l, :, :c_iw].astype(BF16),
            'w_iw': w_in[l, :, c_iw:c_ik].astype(BF16),
            'w_ik': w_in[l, :, c_ik:c_xl].astype(BF16),
            'w_tail': w_in[l, :, c_xl:].astype(BF16),
            'conv_w': conv_w[l], 'conv_b': conv_b[l],
            'lru_wa': lru_wa[l], 'wa_bf': lru_wa[l].astype(BF16), 'wx_bf': lru_wx[l].astype(BF16),
            'lru_ba': lru_ba[l], 'lru_bx': lru_bx[l], 'lru_lambda': lru_lambda[l],
            'wb1': w_branch[l, :aw].astype(BF16), 'wb2': w_branch[l, aw:].astype(BF16),
            'w_out': w_out[l].astype(BF16), 'w_gate': w_gate[l].astype(BF16),
            'w_up': w_up[l].astype(BF16), 'w_down': w_down[l].astype(BF16),
        }
        xp, xs, new_p, new_s = _layer(xp, xs, dims, l, caches, (state_lru[l], state_conv[l]), p)
        news_p.append(new_p)
        news_s.append(new_s)

    stack = lambda news, i: jnp.stack([n[i] for n in news])
    return (xp.reshape(bp, tp, d), xs.reshape(bs, ts, d),
            stack(news_p, 0), stack(news_p, 1), stack(news_p, 2), stack(news_p, 3), stack(news_p, 4),
            stack(news_s, 0), stack(news_s, 1), stack(news_s, 2), stack(news_s, 3), stack(news_s, 4))
```

```python
import functools
import math

import numpy as np
import jax
import jax.numpy as jnp
from jax import lax
from jax.experimental import pallas as pl
from jax.experimental.pallas import tpu as pltpu

CHUNK = 64
TOPK_MAX = 256
ROPE_THETA = 500000.0
CONV_W = 4
LRU_C = 8.0
EPS = 1e-6
NEG = -1e30

LANES = 128
SUBLANES = 8
BF16_ROWS = 16
V7X_VMEM_LIMIT_BYTES = 56 * 1024 * 1024

F32 = jnp.float32
BF16 = jnp.bfloat16
I32 = jnp.int32
INT_MIN = -(2 ** 31)
_NT = (((1,), (1,)), ((), ()))


def _params(n_grid):
    return pltpu.CompilerParams(
        dimension_semantics=("arbitrary",) * n_grid,
        vmem_limit_bytes=V7X_VMEM_LIMIT_BYTES)


def _row_tile(m, target):
    best = None
    for t in range(BF16_ROWS, min(m, target) + 1, BF16_ROWS):
        if m % t == 0:
            best = t
    assert best is not None, (m, target)
    return best


def _col_tile(sizes, target):
    t = target
    while t >= LANES:
        if all(s % t == 0 for s in sizes):
            return t
        t -= LANES
    raise ValueError(sizes)


def _sort_key(x):
    bits = pltpu.bitcast(x, I32)
    return bits ^ ((bits >> 31) & jnp.int32(0x7FFFFFFF))


def _np_sort_key(v):
    bits = int(np.array(v, np.float32).view(np.int32))
    return bits ^ ((bits >> 31) & 0x7FFFFFFF)


def _rmsnorm_kernel(xp_ref, xs_ref, g_ref, o_ref, *, n_prompt_blocks):
    def norm(x_ref):
        x = x_ref[...]
        ms = jnp.mean(x * x, axis=-1, keepdims=True)
        o_ref[...] = (x * lax.rsqrt(ms + EPS) * g_ref[...]).astype(o_ref.dtype)

    @pl.when(pl.program_id(0) < n_prompt_blocks)
    def _():
        norm(xp_ref)

    @pl.when(pl.program_id(0) >= n_prompt_blocks)
    def _():
        norm(xs_ref)


def _rmsnorm(xp, xs, g):
    mp, d = xp.shape
    ms = xs.shape[0]
    tm = _row_tile(math.gcd(mp, ms), 256)
    npb = mp // tm
    return pl.pallas_call(
        functools.partial(_rmsnorm_kernel, n_prompt_blocks=npb),
        grid=((mp + ms) // tm,),
        in_specs=[pl.BlockSpec((tm, d), lambda i: (jnp.minimum(i, npb - 1), 0)),
                  pl.BlockSpec((tm, d), lambda i: (jnp.maximum(i - npb, 0), 0)),
                  pl.BlockSpec((1, d), lambda i: (0, 0))],
        out_specs=pl.BlockSpec((tm, d), lambda i: (i, 0)),
        out_shape=jax.ShapeDtypeStruct((mp + ms, d), BF16),
        compiler_params=_params(1),
        name="rmsnorm",
    )(xp, xs, g.reshape(1, d))


def _cast_weight_once(w_ref, wb_ref):
    @pl.when(pl.program_id(1) == 0)
    def _():
        wb_ref[...] = w_ref[...].astype(wb_ref.dtype)


def _proj_kernel(*refs, norm, rope, scale, hd, half):
    it = iter(refs)
    a_ref, w_ref = next(it), next(it)
    g_ref = next(it) if norm else None
    c_ref, s_ref = (next(it), next(it)) if rope else (None, None)
    rest = list(it)
    outs, wb_ref = rest[:-1], rest[-1]
    _cast_weight_once(w_ref.at[0], wb_ref)
    acc = lax.dot_general(a_ref[...], wb_ref[...], _NT, preferred_element_type=F32)
    if scale is not None:
        acc = acc * scale
    if not (norm or rope):
        for o in outs:
            o[...] = acc.astype(o.dtype)
        return
    tm, tn = acc.shape
    if rope:
        first = lax.broadcasted_iota(I32, (tm, hd), 1) < half
    for h in range(tn // hd):
        xh = acc[:, h * hd:(h + 1) * hd]
        if norm:
            ms = jnp.mean(xh * xh, axis=-1, keepdims=True)
            xh = xh * lax.rsqrt(ms + EPS) * g_ref[...]
        if rope:
            partner = jnp.where(first, pltpu.roll(xh, hd - half, 1), pltpu.roll(xh, half, 1))
            xh = xh * c_ref[...] + partner * s_ref[...]
        for o in outs:
            o[:, h * hd:(h + 1) * hd] = xh.astype(o.dtype)


def _project(a, wt, layer, col_off, n, tn, out_dtypes, *, rows=None, gain=None, rope_tabs=None,
             scale=None, hd=LANES):
    k = a.shape[1]
    row0, m = rows if rows is not None else (0, a.shape[0])
    tm = _row_tile(math.gcd(m, row0) if row0 else m, 1088)
    ro = row0 // tm
    assert col_off % SUBLANES == 0 and n % tn == 0
    in_specs = [pl.BlockSpec((tm, k), lambda j, i: (ro + i, 0)),
                pl.BlockSpec((pl.Element(1), pl.Element(tn), pl.Element(k)),
                             lambda j, i: (layer, pl.multiple_of(col_off + j * tn, SUBLANES), 0))]
    args = [a, wt]
    if gain is not None:
        in_specs.append(pl.BlockSpec((1, hd), lambda j, i: (0, 0)))
        args.append(gain.reshape(1, hd))
    if rope_tabs is not None:
        in_specs += [pl.BlockSpec((tm, hd), lambda j, i: (ro + i, 0))] * 2
        args += list(rope_tabs)
    kern = functools.partial(_proj_kernel, norm=gain is not None, rope=rope_tabs is not None,
                             scale=scale, hd=hd, half=hd // 8)
    outs = pl.pallas_call(
        kern,
        grid=(n // tn, m // tm),
        in_specs=in_specs,
        out_specs=[pl.BlockSpec((tm, tn), lambda j, i: (i, j)) for _ in out_dtypes],
        out_shape=[jax.ShapeDtypeStruct((m, n), dt) for dt in out_dtypes],
        scratch_shapes=[pltpu.VMEM((tn, k), BF16)],
        compiler_params=_params(2),
        name="in_proj",
    )(*args)
    return outs


def _select_to_bias(key_ref, bias_ref, lim_ref, nblk, tks, topk, idx_bits, key_valid_min, axis):
    nq = key_ref.shape[1 - axis]
    static = isinstance(nblk, int)
    kf = float(topk)
    blk_shape = (nq, tks) if axis == 1 else (tks, nq)

    def loop(body, init):
        if static:
            c = init
            for j in range(nblk):
                c = body(j, c)
            return c
        return lax.fori_loop(0, nblk, body, init)

    def at(j):
        s = pl.ds(j * tks if static else pl.multiple_of(j * tks, tks), tks)
        return (slice(None), s) if axis == 1 else (s, slice(None))

    def idx(j):
        return j * tks + lax.broadcasted_iota(I32, blk_shape, axis)

    def fold(x):
        if axis == 1:
            p = x[:, :LANES]
            for c in range(1, tks // LANES):
                p = p + x[:, c * LANES:(c + 1) * LANES]
            return p
        return jnp.sum(x.reshape(tks // SUBLANES, SUBLANES, nq), axis=0)

    part_shape = (nq, LANES) if axis == 1 else (SUBLANES, nq)
    vec_shape = (nq, 1) if axis == 1 else (1, nq)

    def count(pred):
        def body(j, c):
            return c + fold(jnp.where(pred(key_ref[at(j)], j), jnp.float32(1), jnp.float32(0)))
        return jnp.sum(loop(body, jnp.zeros(part_shape, F32)), axis=axis, keepdims=True)

    def value_bit(it, thr):
        cand = thr + (jnp.int32(1) << (31 - it))
        return jnp.where(count(lambda kb, j: kb >= cand) >= kf, cand, thr)

    thr = lax.fori_loop(0, 32, value_bit, jnp.full(vec_shape, INT_MIN, I32))
    need = kf - count(lambda kb, j: kb > thr)
    n_ge = count(lambda kb, j: kb >= thr)

    lim_ref[...] = jnp.full(lim_ref.shape, 2 ** idx_bits, I32)

    @pl.when(jnp.max(n_ge) > kf)
    def _():
        def index_bit(it, lim):
            cand = lim + (jnp.int32(1) << (idx_bits - 1 - it))
            n = count(lambda kb, j: jnp.where(kb == thr, idx(j), 2 ** idx_bits) < cand)
            return jnp.where(n <= need, cand, lim)
        lim = lax.fori_loop(0, idx_bits, index_bit, jnp.zeros(vec_shape, I32))
        lim_ref[...] = jnp.broadcast_to(lim, lim_ref.shape)

    lim = lim_ref[:, :1] if axis == 1 else lim_ref[:1, :]

    def write(j, c):
        kb = key_ref[at(j)]
        ok = jnp.where(kb > key_valid_min, 0.0, NEG)
        tie = jnp.where(idx(j) < lim, ok, NEG)
        bias_ref[at(j)] = jnp.where(kb > thr, ok, jnp.where(kb == thr, tie, NEG))
        return c
    loop(write, 0)


def _indexer_scores(iq_ref, iw, ikb, nh, dk):
    acc = None
    for h in range(nh):
        d = lax.dot_general(iq_ref[:, h * dk:(h + 1) * dk], ikb, _NT, preferred_element_type=F32)
        term = jnp.maximum(d, 0.0) * iw[:, h:h + 1]
        acc = term if acc is None else acc + term
    return acc


def _select_prompt_kernel(iq_ref, iwt_ref, ik_ref, bias_ref, key_ref, lim_ref, *,
                          tq, tk, nh, dk, topk, nblk_total, idx_bits, key_valid_min, chunk_shift):
    qi = pl.program_id(1)
    nkv = ((qi + 1) * tq + tk - 1) // tk

    def score_block(j, c):
        off = pl.multiple_of(j * tk, tk)
        ikb = ik_ref[pl.ds(off, tk), :]
        acc = None
        for h in range(nh):
            d = lax.dot_general(ikb, iq_ref[:, h * dk:(h + 1) * dk], _NT, preferred_element_type=F32)
            term = jnp.maximum(d, 0.0) * iwt_ref[h:h + 1, :]
            acc = term if acc is None else acc + term
        kpos = off + lax.broadcasted_iota(I32, (tk, tq), 0)
        qpos = qi * tq + lax.broadcasted_iota(I32, (tk, tq), 1)
        allowed = (kpos >> chunk_shift) <= (qpos >> chunk_shift)
        key_ref[pl.ds(off, tk), :] = _sort_key(jnp.where(allowed, acc, NEG))
        return c
    lax.fori_loop(0, nkv, score_block, 0)

    bias = bias_ref.at[0]
    _select_to_bias(key_ref, bias, lim_ref, nkv, tk, topk, idx_bits, key_valid_min, axis=0)

    def fill(j, c):
        bias[pl.ds(pl.multiple_of(j * tk, tk), tk), :] = jnp.full((tk, tq), NEG, F32)
        return c
    lax.fori_loop(nkv, nblk_total, fill, 0)


def _select_prompt(iq, iwt, ik, nb, t, tq, tk, topk, chunk_shift):
    nh = iwt.shape[0]
    dk = ik.shape[1]
    nq = t // tq
    kern = functools.partial(
        _select_prompt_kernel, tq=tq, tk=tk, nh=nh, dk=dk, topk=topk, nblk_total=t // tk,
        idx_bits=int(t).bit_length(), key_valid_min=_np_sort_key(0.5 * NEG), chunk_shift=chunk_shift)
    return pl.pallas_call(
        kern,
        grid=(nb, nq),
        in_specs=[pl.BlockSpec((tq, nh * dk), lambda b, i: (b * nq + i, 0)),
                  pl.BlockSpec((nh, tq), lambda b, i: (0, b * nq + i)),
                  pl.BlockSpec((t, dk), lambda b, i: (b, 0))],
        out_specs=pl.BlockSpec((1, t, tq), lambda b, i: (b * nq + i, 0, 0)),
        out_shape=jax.ShapeDtypeStruct((nb * nq, t, tq), F32),
        scratch_shapes=[pltpu.VMEM((t, tq), I32), pltpu.VMEM((SUBLANES, tq), I32)],
        compiler_params=_params(2),
        name="select_prompt",
    )(iq, iwt, ik)


def _select_sample_kernel(iq_ref, iw_ref, ikc_ref, ikn_ref, bias_ref, key_ref, m_ref, *,
                          ts, past, sw, nh, dk, topk, idx_bits, key_valid_min, chunk_shift):
    iw = iw_ref[...]
    for j in range(past // sw):
        ikb = ikc_ref[j * sw:(j + 1) * sw, :].astype(BF16)
        acc = _indexer_scores(iq_ref, iw, ikb, nh, dk)
        qpos = past + lax.broadcasted_iota(I32, (ts, sw), 0)
        kpos = j * sw + lax.broadcasted_iota(I32, (ts, sw), 1)
        allowed = (kpos >> chunk_shift) <= (qpos >> chunk_shift)
        key_ref[:, j * sw:(j + 1) * sw] = _sort_key(jnp.where(allowed, acc, NEG))
    key_ref[:, past:past + LANES] = jnp.full((ts, LANES), _np_sort_key(NEG), I32)
    acc = _indexer_scores(iq_ref, iw, ikn_ref[...], nh, dk)
    qpos = past + lax.broadcasted_iota(I32, (ts, ts), 0)
    kpos = past + lax.broadcasted_iota(I32, (ts, ts), 1)
    allowed = (kpos >> chunk_shift) <= (qpos >> chunk_shift)
    key_ref[:, past:past + ts] = _sort_key(jnp.where(allowed, acc, NEG))
    _select_to_bias(key_ref, bias_ref, m_ref, past // LANES + 1, LANES, topk, idx_bits, key_valid_min,
                    axis=1)


def _select_sample(iq, iw, ik_cache, ik_new, row_blk_off, layer, nb, ts, topk, chunk_shift):
    nh = iw.shape[1]
    past, dk = ik_cache.shape[2], ik_cache.shape[3]
    assert past % LANES == 0 and ts <= LANES
    s_pad = past + LANES
    kern = functools.partial(
        _select_sample_kernel, ts=ts, past=past, sw=_col_tile((past,), 512), nh=nh, dk=dk, topk=topk,
        idx_bits=int(s_pad).bit_length(), key_valid_min=_np_sort_key(0.5 * NEG), chunk_shift=chunk_shift)
    return pl.pallas_call(
        kern,
        grid=(nb,),
        in_specs=[pl.BlockSpec((ts, nh * dk), lambda b: (row_blk_off + b, 0)),
                  pl.BlockSpec((ts, nh), lambda b: (row_blk_off + b, 0)),
                  pl.BlockSpec((None, None, past, dk), lambda b: (layer, b, 0, 0)),
                  pl.BlockSpec((ts, dk), lambda b: (b, 0))],
        out_specs=pl.BlockSpec((ts, s_pad), lambda b: (b, 0)),
        out_shape=jax.ShapeDtypeStruct((nb * ts, s_pad), F32),
        scratch_shapes=[pltpu.VMEM((ts, s_pad), I32), pltpu.VMEM((ts, LANES), I32)],
        compiler_params=_params(1),
        name="select_sample",
    )(iq, iw, ik_cache, ik_new)


def _attn_prompt_kernel(q_ref, k_ref, vt_ref, bias_ref, o_ref, m_ref, l_ref, acc_ref, s_ref, mc_ref, *,
                        tq, tk, grp, hd, scale_log2e):
    nkv = ((pl.program_id(1) + 1) * tq + tk - 1) // tk
    m_ref[...] = jnp.full(m_ref.shape, -jnp.inf, F32)
    l_ref[...] = jnp.zeros(l_ref.shape, F32)
    acc_ref[...] = jnp.zeros(acc_ref.shape, F32)

    def body(j, c):
        off = pl.multiple_of(j * tk, tk)
        kb = k_ref[pl.ds(off, tk), :]
        vtb = vt_ref[:, pl.ds(off, tk)]
        bias = bias_ref[0, pl.ds(off, tk), :]
        for hh in range(grp):
            qh = q_ref[:, hh * hd:(hh + 1) * hd]
            s = lax.dot_general(kb, qh, _NT, preferred_element_type=F32) * scale_log2e + bias
            s_ref[hh] = s
            mc_ref[hh] = jnp.max(s, axis=0, keepdims=True)
        for hh in range(grp):
            s = s_ref[hh]
            m = m_ref[hh]
            m_new = jnp.maximum(m, mc_ref[hh])
            alpha = jnp.exp2(m - m_new)
            p = jnp.exp2(s - m_new)
            l_ref[hh] = alpha * l_ref[hh] + jnp.sum(p, axis=0, keepdims=True)
            acc_ref[hh] = alpha * acc_ref[hh] + jnp.dot(vtb, p.astype(BF16),
                                                        preferred_element_type=F32)
            m_ref[hh] = m_new
        return c
    lax.fori_loop(0, nkv, body, 0)

    for hh in range(grp):
        o_ref[:, hh * hd:(hh + 1) * hd] = (acc_ref[hh] / l_ref[hh]).T.astype(o_ref.dtype)


def _attn_prompt(q, k, vt, bias_t, nb, t, tq, tk, kvh, hd):
    aw = q.shape[1]
    grp = aw // (kvh * hd)
    nq = t // tq
    kern = functools.partial(_attn_prompt_kernel, tq=tq, tk=tk, grp=grp, hd=hd,
                             scale_log2e=hd ** -0.5 * math.log2(math.e))
    return pl.pallas_call(
        kern,
        grid=(nb, nq, kvh),
        in_specs=[pl.BlockSpec((tq, grp * hd), lambda b, i, g: (b * nq + i, g)),
                  pl.BlockSpec((t, hd), lambda b, i, g: (b, g)),
                  pl.BlockSpec((hd, t), lambda b, i, g: (g, b)),
                  pl.BlockSpec((1, t, tq), lambda b, i, g: (b * nq + i, 0, 0))],
        out_specs=pl.BlockSpec((tq, grp * hd), lambda b, i, g: (b * nq + i, g)),
        out_shape=jax.ShapeDtypeStruct((nb * t, aw), BF16),
        scratch_shapes=[pltpu.VMEM((grp, 1, tq), F32),
                        pltpu.VMEM((grp, 1, tq), F32),
                        pltpu.VMEM((grp, hd, tq), F32),
                        pltpu.VMEM((grp, tk, tq), F32),
                        pltpu.VMEM((grp, 1, tq), F32)],
        compiler_params=_params(3),
        name="attn_prompt",
    )(q, k, vt, bias_t)


def _attn_sample_kernel(q_ref, kc_ref, vc_ref, kn_ref, vn_ref, bias_ref, o_ref, *,
                        ts, past, kvh, grp, hd, scale):
    bias_c = bias_ref[:, :past]
    bias_n = bias_ref[:, past:past + ts]
    for g in range(kvh):
        kc = kc_ref[pl.ds(g, past, stride=kvh), :].astype(BF16)
        vc = vc_ref[pl.ds(g, past, stride=kvh), :].astype(BF16)
        kn = kn_ref[:, g * hd:(g + 1) * hd]
        vn = vn_ref[:, g * hd:(g + 1) * hd]
        for hh in range(g * grp, (g + 1) * grp):
            qh = q_ref[:, hh * hd:(hh + 1) * hd]
            sc = lax.dot_general(qh, kc, _NT, preferred_element_type=F32) * scale + bias_c
            sn = lax.dot_general(qh, kn, _NT, preferred_element_type=F32) * scale + bias_n
            m = jnp.maximum(jnp.max(sc, axis=-1, keepdims=True), jnp.max(sn, axis=-1, keepdims=True))
            pc = jnp.exp(sc - m)
            pn = jnp.exp(sn - m)
            l = jnp.sum(pc, axis=-1, keepdims=True) + jnp.sum(pn, axis=-1, keepdims=True)
            acc = (jnp.dot(pc.astype(BF16), vc, preferred_element_type=F32)
                   + jnp.dot(pn.astype(BF16), vn, preferred_element_type=F32))
            o_ref[:, hh * hd:(hh + 1) * hd] = (acc / l).astype(o_ref.dtype)


def _attn_sample(q, k_cache, v_cache, k_new, v_new, bias, row_blk_off, layer, nb, ts, kvh, hd):
    aw = q.shape[1]
    grp = aw // (kvh * hd)
    past = k_cache.shape[2]
    depth = k_cache.shape[0]
    k_cache = k_cache.reshape(depth, nb, past * kvh, hd)
    v_cache = v_cache.reshape(depth, nb, past * kvh, hd)
    cache_spec = lambda: pl.BlockSpec((None, None, past * kvh, hd), lambda b: (layer, b, 0, 0))
    kern = functools.partial(_attn_sample_kernel, ts=ts, past=past, kvh=kvh, grp=grp, hd=hd,
                             scale=hd ** -0.5)
    return pl.pallas_call(
        kern,
        grid=(nb,),
        in_specs=[pl.BlockSpec((ts, aw), lambda b: (row_blk_off + b, 0)),
                  cache_spec(), cache_spec(),
                  pl.BlockSpec((ts, kvh * hd), lambda b: (b, 0)),
                  pl.BlockSpec((ts, kvh * hd), lambda b: (b, 0)),
                  pl.BlockSpec((ts, past + LANES), lambda b: (b, 0))],
        out_specs=pl.BlockSpec((ts, aw), lambda b: (b, 0)),
        out_shape=jax.ShapeDtypeStruct((nb * ts, aw), BF16),
        compiler_params=_params(1),
        name="attn_sample",
    )(q, k_cache, v_cache, k_new, v_new, bias)


def _lru_kernel(xl_ref, yl_ref, c0_ref, h0_ref, cw_ref, cb_ref, wa_ref, ba_ref, wx_ref, bx_ref,
                lam_ref, o_ref, hl_ref, cn_ref, xbuf, hc, a_s, b_s, h_s, *, tt, cw, bw):
    ti = pl.program_id(2)
    pad = SUBLANES
    tail = CONV_W - 1

    @pl.when(ti == 0)
    def _():
        xbuf[pad - tail:pad, :] = c0_ref[0]
        hc[...] = jnp.broadcast_to(h0_ref[0], hc.shape)
        a_s[:pad, :] = jnp.ones((pad, cw), F32)
        b_s[:pad, :] = jnp.zeros((pad, cw), F32)

    x = xl_ref[...]
    xbuf[pad:, :] = x
    u = cb_ref[...]
    for j in range(CONV_W):
        u = u + xbuf[pad - tail + j:pad - tail + j + tt, :] * cw_ref[j:j + 1, :]
    xbuf[pad - tail:pad, :] = xbuf[pad + tt - tail:pad + tt, :]

    @pl.when(ti == pl.num_programs(2) - 1)
    def _():
        cn_ref[0] = x[tt - tail:, :]

    ub = u.astype(BF16)
    for n in range(cw // bw):
        cs = slice(n * bw, (n + 1) * bw)
        un = ub[:, cs]
        r = jax.nn.sigmoid(jnp.dot(un, wa_ref[n], preferred_element_type=F32) + ba_ref[:, cs])
        i = jax.nn.sigmoid(jnp.dot(un, wx_ref[n], preferred_element_type=F32) + bx_ref[:, cs])
        log_a = -LRU_C * r * jax.nn.softplus(-lam_ref[:, cs])
        a_s[pad:, cs] = jnp.exp(log_a)
        t = jnp.tanh(log_a)
        b_s[pad:, cs] = jnp.sqrt(-2.0 * t / (1.0 - t)) * (i * u[:, cs])

    row = lax.broadcasted_iota(I32, (tt, cw), 0) & (SUBLANES - 1)
    a = a_s[pad:, :]
    b = b_s[pad:, :]
    d = 1
    while d < SUBLANES:
        keep = row >= d
        a_sh = jnp.where(keep, a_s[pad - d:pad - d + tt, :], 1.0)
        b_sh = jnp.where(keep, b_s[pad - d:pad - d + tt, :], 0.0)
        b = a * b_sh + b
        a = a * a_sh
        a_s[pad:, :] = a
        b_s[pad:, :] = b
        d *= 2

    carry = hc[...]
    for g in range(tt // SUBLANES):
        rs = slice(pad + g * SUBLANES, pad + (g + 1) * SUBLANES)
        h = a_s[rs, :] * carry + b_s[rs, :]
        h_s[g * SUBLANES:(g + 1) * SUBLANES, :] = h
        carry = jnp.broadcast_to(h[SUBLANES - 1:, :], hc.shape)
    hc[...] = carry

    @pl.when(ti == pl.num_programs(2) - 1)
    def _():
        hl_ref[0] = carry[:1, :]

    o_ref[...] = (h_s[...] * jax.nn.gelu(yl_ref[...])).astype(o_ref.dtype)


def _lru_branch(tail_proj, xl_col, yl_col, row_off, conv0, h0, p, nb, t):
    r = conv0.shape[-1]
    bw = p['lru_wa'].shape[-1]
    cw = bw * max(1, min(r // bw, 512 // bw))
    tt = t if t <= 256 else 256
    assert t % tt == 0 and r % cw == 0 and row_off % tt == 0 and tt % SUBLANES == 0
    nt = t // tt
    ro = row_off // tt
    xo, yo = xl_col // cw, yl_col // cw
    assert xl_col % cw == 0 and yl_col % cw == 0
    nblk = cw // bw
    kern = functools.partial(_lru_kernel, tt=tt, cw=cw, bw=bw)
    vec = lambda: pl.BlockSpec((1, cw), lambda b, c, i: (0, c))
    return pl.pallas_call(
        kern,
        grid=(nb, r // cw, nt),
        in_specs=[pl.BlockSpec((tt, cw), lambda b, c, i: (ro + b * nt + i, xo + c)),
                  pl.BlockSpec((tt, cw), lambda b, c, i: (ro + b * nt + i, yo + c)),
                  pl.BlockSpec((1, CONV_W - 1, cw), lambda b, c, i: (b, 0, c)),
                  pl.BlockSpec((1, 1, cw), lambda b, c, i: (b, 0, c)),
                  pl.BlockSpec((CONV_W, cw), lambda b, c, i: (0, c)),
                  vec(),
                  pl.BlockSpec((nblk, bw, bw), lambda b, c, i: (c, 0, 0)),
                  vec(),
                  pl.BlockSpec((nblk, bw, bw), lambda b, c, i: (c, 0, 0)),
                  vec(), vec()],
        out_specs=[pl.BlockSpec((tt, cw), lambda b, c, i: (b * nt + i, c)),
                   pl.BlockSpec((1, 1, cw), lambda b, c, i: (b, 0, c)),
                   pl.BlockSpec((1, CONV_W - 1, cw), lambda b, c, i: (b, 0, c))],
        out_shape=[jax.ShapeDtypeStruct((nb * t, r), BF16),
                   jax.ShapeDtypeStruct((nb, 1, r), F32),
                   jax.ShapeDtypeStruct((nb, CONV_W - 1, r), F32)],
        scratch_shapes=[pltpu.VMEM((tt + SUBLANES, cw), F32),
                        pltpu.VMEM((SUBLANES, cw), F32),
                        pltpu.VMEM((tt + SUBLANES, cw), F32),
                        pltpu.VMEM((tt + SUBLANES, cw), F32),
                        pltpu.VMEM((tt, cw), F32)],
        compiler_params=_params(3),
        name="conv_rglru",
    )(tail_proj, tail_proj, conv0, h0.reshape(nb, 1, r), p['conv_w'], p['conv_b'].reshape(1, r),
      p['wa_bf'], p['lru_ba'].reshape(1, r), p['wx_bf'], p['lru_bx'].reshape(1, r),
      p['lru_lambda'].reshape(1, r))


def _mix_kernel(oa_ref, ol_ref, w_ref, ga_ref, gl_ref, o_ref, w1b_ref, w2b_ref, *, ka):
    @pl.when(pl.program_id(1) == 0)
    def _():
        w1b_ref[...] = w_ref[:ka, :].astype(BF16)
        w2b_ref[...] = w_ref[ka:, :].astype(BF16)
    ya = jnp.dot(oa_ref[...], w1b_ref[...], preferred_element_type=F32)
    yl = jnp.dot(ol_ref[...], w2b_ref[...], preferred_element_type=F32)
    mixed = jax.nn.sigmoid(ga_ref[...]) * ya + jax.nn.sigmoid(gl_ref[...]) * yl
    o_ref[...] = mixed.astype(o_ref.dtype)


def _mix(oa, ol, w, layer, tail_proj, row0, ga_col, gl_col):
    m, ka = oa.shape
    kl = ol.shape[1]
    n = w.shape[2]
    tm = _row_tile(math.gcd(m, row0) if row0 else m, 512)
    ro = row0 // tm
    tn = _col_tile((n, ga_col, gl_col), 512)
    go, lo = ga_col // tn, gl_col // tn
    return pl.pallas_call(
        functools.partial(_mix_kernel, ka=ka),
        grid=(n // tn, m // tm),
        in_specs=[pl.BlockSpec((tm, ka), lambda j, i: (i, 0)),
                  pl.BlockSpec((tm, kl), lambda j, i: (i, 0)),
                  pl.BlockSpec((None, ka + kl, tn), lambda j, i: (layer, 0, j),
                               pipeline_mode=pl.Buffered(1)),
                  pl.BlockSpec((tm, tn), lambda j, i: (ro + i, go + j)),
                  pl.BlockSpec((tm, tn), lambda j, i: (ro + i, lo + j))],
        out_specs=pl.BlockSpec((tm, tn), lambda j, i: (i, j)),
        out_shape=jax.ShapeDtypeStruct((m, n), BF16),
        scratch_shapes=[pltpu.VMEM((ka, tn), BF16), pltpu.VMEM((kl, tn), BF16)],
        compiler_params=_params(2),
        name="branch_mix",
    )(oa, ol, w, tail_proj, tail_proj)


def _out_proj_kernel(a_ref, w_ref, x_ref, o_ref, wb_ref):
    _cast_weight_once(w_ref, wb_ref)
    o_ref[...] = x_ref[...] + jnp.dot(a_ref[...], wb_ref[...], preferred_element_type=F32)


def _out_proj(a, w, layer, x):
    m, k = a.shape
    n = w.shape[2]
    tm = _row_tile(m, 1024)
    tn = _col_tile((n,), 512)
    return pl.pallas_call(
        _out_proj_kernel,
        grid=(n // tn, m // tm),
        in_specs=[pl.BlockSpec((tm, k), lambda j, i: (i, 0)),
                  pl.BlockSpec((None, k, tn), lambda j, i: (layer, 0, j)),
                  pl.BlockSpec((tm, tn), lambda j, i: (i, j))],
        out_specs=pl.BlockSpec((tm, tn), lambda j, i: (i, j)),
        out_shape=jax.ShapeDtypeStruct((m, n), F32),
        scratch_shapes=[pltpu.VMEM((k, tn), BF16)],
        compiler_params=_params(2),
        name="out_proj",
    )(a, w, x)


def _residual_mm_kernel(a_ref, w_ref, x_ref, o_ref):
    o_ref[...] = x_ref[...] + jnp.dot(a_ref[...], w_ref[...], preferred_element_type=F32)


def _residual_mm(a, row0, w, x, tm_target, tn_target):
    k = a.shape[1]
    m = x.shape[0]
    n = w.shape[1]
    tm = _row_tile(math.gcd(m, row0) if row0 else m, tm_target)
    ro = row0 // tm
    tn = _col_tile((n,), tn_target)
    return pl.pallas_call(
        _residual_mm_kernel,
        grid=(m // tm, n // tn),
        in_specs=[pl.BlockSpec((tm, k), lambda i, j: (ro + i, 0)),
                  pl.BlockSpec((k, tn), lambda i, j: (0, j)),
                  pl.BlockSpec((tm, tn), lambda i, j: (i, j))],
        out_specs=pl.BlockSpec((tm, tn), lambda i, j: (i, j)),
        out_shape=jax.ShapeDtypeStruct((m, n), F32),
        compiler_params=_params(2),
        name="residual_matmul",
    )(a, w, x)


def _swiglu_kernel(a_ref, wg_ref, wu_ref, o_ref, wgb_ref, wub_ref):
    _cast_weight_once(wg_ref, wgb_ref)
    _cast_weight_once(wu_ref, wub_ref)
    a = a_ref[...]
    g = jnp.dot(a, wgb_ref[...], preferred_element_type=F32)
    u = jnp.dot(a, wub_ref[...], preferred_element_type=F32)
    o_ref[...] = (jax.nn.silu(g) * u).astype(o_ref.dtype)


def _swiglu(a, wg, wu, layer):
    m, k = a.shape
    n = wg.shape[2]
    tm = _row_tile(m, 1088)
    tn = _col_tile((n,), 256)
    wspec = lambda: pl.BlockSpec((None, k, tn), lambda j, i: (layer, 0, j))
    return pl.pallas_call(
        _swiglu_kernel,
        grid=(n // tn, m // tm),
        in_specs=[pl.BlockSpec((tm, k), lambda j, i: (i, 0)), wspec(), wspec()],
        out_specs=pl.BlockSpec((tm, tn), lambda j, i: (i, j)),
        out_shape=jax.ShapeDtypeStruct((m, n), BF16),
        scratch_shapes=[pltpu.VMEM((k, tn), BF16), pltpu.VMEM((k, tn), BF16)],
        compiler_params=_params(2),
        name="swiglu",
    )(a, wg, wu)


def _rope_tables(pos, hd):
    rot = hd // 4
    half = rot // 2
    inv = ROPE_THETA ** (-jnp.arange(half, dtype=F32) * 2.0 / rot)
    ang = pos.astype(F32)[:, None] * inv[None, :]
    cos, sin = jnp.cos(ang), jnp.sin(ang)
    n = pos.shape[0]
    c = jnp.concatenate([cos, cos, jnp.ones((n, hd - rot), F32)], axis=-1)
    s = jnp.concatenate([-sin, sin, jnp.zeros((n, hd - rot), F32)], axis=-1)
    return c, s


def _layer(xp, xs, dims, layer, caches, states, p):
    bp, tp, bs, ts, past, kvh, hd, di, ih = dims
    cache_k, cache_v, cache_ik = caches
    state_lru, state_conv = states
    d = xp.shape[1]
    mp, ms = bp * tp, bs * ts
    r = state_lru.shape[-1]
    aw = p['w_branch'].shape[1] - r
    kvw = kvh * hd
    assert hd == LANES and di == LANES and mp % ts == 0
    chunk_shift = CHUNK.bit_length() - 1
    assert 1 << chunk_shift == CHUNK
    rows_p, rows_s = (0, mp), (mp, ms)

    pos = jnp.concatenate([jnp.tile(jnp.arange(tp, dtype=I32), bp),
                           jnp.tile(past + jnp.arange(ts, dtype=I32), bs)])
    tabs = _rope_tables(pos, hd)

    xn = _rmsnorm(xp, xs, p['norm_mix'])
    tn = _col_tile((aw, kvw, ih * di), 512)
    c_k, c_v, c_iq = aw, aw + kvw, aw + 2 * kvw
    w_in = p['w_in_t']
    c_iw = c_iq + ih * di
    c_ik = c_iw + ih
    c_xl = c_ik + di
    (q_bf,) = _project(xn, w_in, layer, 0, aw, tn, (BF16,), gain=p['norm_q'], rope_tabs=tabs)
    (iq_bf,) = _project(xn, w_in, layer, c_iq, ih * di, tn, (BF16,), rope_tabs=tabs)
    (iw,) = _project(xn, w_in, layer, c_iw, ih, ih, (F32,), scale=float((ih * di) ** -0.5))
    tn_tail = _col_tile((r, d), 512)
    (tail,) = _project(xn, w_in, layer, c_xl, 2 * r + 2 * d, tn_tail, (F32,))
    xl_col, yl_col, ga_col, gl_col = 0, r, 2 * r, 2 * r + d
    kvi = []
    for rows in (rows_p, rows_s):
        k_f, k_bf = _project(xn, w_in, layer, c_k, kvw, tn, (F32, BF16), rows=rows,
                             gain=p['norm_k'], rope_tabs=tabs)
        v_f, v_bf = _project(xn, w_in, layer, c_v, kvw, tn, (F32, BF16), rows=rows)
        ik_f, ik_bf = _project(xn, w_in, layer, c_ik, di, di, (F32, BF16), rows=rows,
                               gain=p['norm_idx_k'], rope_tabs=tabs)
        kvi.append((k_f, k_bf, v_f, v_bf, ik_f, ik_bf))
    (kp_f, kp_bf, vp_f, vp_bf, ikp_f, ikp_bf), (ks_f, ks_bf, vs_f, vs_bf, iks_f, iks_bf) = kvi

    tq = 256 if tp % 256 == 0 else LANES
    tk = 512 if tp % 512 == 0 else tq
    bias_p = _select_prompt(iq_bf, iw.T, ikp_bf, bp, tp, tq, tk, min(TOPK_MAX, tp // 4), chunk_shift)
    oa_p = _attn_prompt(q_bf, kp_bf, vp_bf.T, bias_p, bp, tp, tq, tk, kvh, hd)
    sblk = mp // ts
    bias_s = _select_sample(iq_bf, iw, cache_ik, iks_bf, sblk, layer, bs, ts,
                            min(TOPK_MAX, (past + ts) // 4), chunk_shift)
    oa_s = _attn_sample(q_bf, cache_k, cache_v, ks_bf, vs_bf, bias_s, sblk, layer, bs, ts, kvh, hd)

    ol_p, h_p, c_p = _lru_branch(tail, xl_col, yl_col, 0, jnp.zeros((bp, CONV_W - 1, r), F32),
                                 jnp.zeros((bp, r), F32), p, bp, tp)
    ol_s, h_s, c_s = _lru_branch(tail, xl_col, yl_col, mp, state_conv, state_lru, p, bs, ts)

    x1 = []
    for oa, ol, x, row0 in ((oa_p, ol_p, xp, 0), (oa_s, ol_s, xs, mp)):
        mixed = _mix(oa, ol, p['w_branch'], layer, tail, row0, ga_col, gl_col)
        x1.append(_out_proj(mixed, p['w_out'], layer, x))
    xn2 = _rmsnorm(x1[0], x1[1], p['norm_ffn'])
    hff = _swiglu(xn2, p['w_gate'], p['w_up'], layer)
    yp = _residual_mm(hff, 0, p['w_down_bf'], x1[0], 512, 256)
    ys = _residual_mm(hff, mp, p['w_down_bf'], x1[1], 512, 256)

    new_p = (kp_f.reshape(bp, tp, kvh, hd), vp_f.reshape(bp, tp, kvh, hd),
             ikp_f.reshape(bp, tp, di), h_p.reshape(bp, r), c_p)
    new_s = (ks_f.reshape(bs, ts, kvh, hd), vs_f.reshape(bs, ts, kvh, hd),
             iks_f.reshape(bs, ts, di), h_s.reshape(bs, r), c_s)
    return yp, ys, new_p, new_s


def kernel(x_prompt, x_sample, cache_k, cache_v, cache_idx_k, state_lru, state_conv, norm_mix, w_in,
           norm_q, norm_k, norm_idx_k, conv_w, conv_b, lru_wa, lru_ba, lru_wx, lru_bx, lru_lambda,
           w_branch, w_out, norm_ffn, w_gate, w_up, w_down):
    depth = w_in.shape[0]
    bp, tp, d = x_prompt.shape
    bs, ts, _ = x_sample.shape
    past, kvh, hd = cache_k.shape[2], cache_k.shape[3], cache_k.shape[4]
    di = cache_idx_k.shape[-1]
    r = state_lru.shape[-1]
    aw = w_branch.shape[1] - r
    kvw = kvh * hd
    ih = (w_in.shape[-1] - aw - 2 * kvw - di - 2 * r - 2 * d) // (di + 1)
    dims =(bp, tp, bs, ts, past, kvh, hd, di, ih)
    xp, xs = x_prompt.reshape(bp * tp, d), x_sample.reshape(bs * ts, d)
    caches = (cache_k, cache_v, cache_idx_k)
    news_p, news_s = [], []
    for l in range(depth):
        p = {
            'norm_mix': norm_mix[l], 'norm_q': norm_q[l], 'norm_k': norm_k[l],
            'norm_idx_k': norm_idx_k[l], 'norm_ffn': norm_ffn[l],
            'w_in_t': jnp.swapaxes(w_in, 1, 2),
            'conv_w': conv_w[l], 'conv_b': conv_b[l],
            'lru_wa': lru_wa[l], 'wa_bf': lru_wa[l].astype(BF16), 'wx_bf': lru_wx[l].astype(BF16),
            'lru_ba': lru_ba[l], 'lru_bx': lru_bx[l], 'lru_lambda': lru_lambda[l],
            'w_branch': w_branch, 'w_out': w_out, 'w_gate': w_gate, 'w_up': w_up,
            'w_down_bf': w_down[l].astype(BF16),
        }
        xp, xs, new_p, new_s = _layer(xp, xs, dims, l, caches, (state_lru[l], state_conv[l]), p)
        news_p.append(new_p)
        news_s.append(new_s)

    stack = lambda news, i: jnp.stack([n[i] for n in news])
    return (xp.reshape(bp, tp, d), xs.reshape(bs, ts, d),
            stack(news_p, 0), stack(news_p, 1), stack(news_p, 2), stack(news_p, 3), stack(news_p, 4),
            stack(news_s, 0), stack(news_s, 1), stack(news_s, 2), stack(news_s, 3), stack(news_s, 4))
```

```python
import functools
import math

import numpy as np
import jax
import jax.numpy as jnp
from jax import lax
from jax.experimental import pallas as pl
from jax.experimental.pallas import tpu as pltpu

CHUNK = 64
TOPK_MAX = 256
ROPE_THETA = 500000.0
CONV_W = 4
LRU_C = 8.0
EPS = 1e-6
NEG = -1e30

LANES = 128
SUBLANES = 8
BF16_ROWS = 16
V7X_VMEM_LIMIT_BYTES = 56 * 1024 * 1024

F32 = jnp.float32
BF16 = jnp.bfloat16
I32 = jnp.int32
INT_MIN = -(2 ** 31)
_NT = (((1,), (1,)), ((), ()))


def _params(n_grid):
    return pltpu.CompilerParams(
        dimension_semantics=("arbitrary",) * n_grid,
        vmem_limit_bytes=V7X_VMEM_LIMIT_BYTES)


def _row_tile(m, target):
    best = None
    for t in range(BF16_ROWS, min(m, target) + 1, BF16_ROWS):
        if m % t == 0:
            best = t
    assert best is not None, (m, target)
    return best


def _col_tile(sizes, target):
    t = target
    while t >= LANES:
        if all(s % t == 0 for s in sizes):
            return t
        t -= LANES
    raise ValueError(sizes)


def _sort_key(x):
    bits = pltpu.bitcast(x, I32)
    return bits ^ ((bits >> 31) & jnp.int32(0x7FFFFFFF))


def _np_sort_key(v):
    bits = int(np.array(v, np.float32).view(np.int32))
    return bits ^ ((bits >> 31) & 0x7FFFFFFF)


def _rmsnorm_kernel(xp_ref, xs_ref, g_ref, o_ref, *, n_prompt_blocks):
    def norm(x_ref):
        x = x_ref[...]
        ms = jnp.mean(x * x, axis=-1, keepdims=True)
        o_ref[...] = (x * lax.rsqrt(ms + EPS) * g_ref[...]).astype(o_ref.dtype)

    @pl.when(pl.program_id(0) < n_prompt_blocks)
    def _():
        norm(xp_ref)

    @pl.when(pl.program_id(0) >= n_prompt_blocks)
    def _():
        norm(xs_ref)


def _rmsnorm(xp, xs, g):
    mp, d = xp.shape
    ms = xs.shape[0]
    tm = _row_tile(math.gcd(mp, ms), 256)
    npb = mp // tm
    return pl.pallas_call(
        functools.partial(_rmsnorm_kernel, n_prompt_blocks=npb),
        grid=((mp + ms) // tm,),
        in_specs=[pl.BlockSpec((tm, d), lambda i: (jnp.minimum(i, npb - 1), 0)),
                  pl.BlockSpec((tm, d), lambda i: (jnp.maximum(i - npb, 0), 0)),
                  pl.BlockSpec((1, d), lambda i: (0, 0))],
        out_specs=pl.BlockSpec((tm, d), lambda i: (i, 0)),
        out_shape=jax.ShapeDtypeStruct((mp + ms, d), BF16),
        compiler_params=_params(1),
        name="rmsnorm",
    )(xp, xs, g.reshape(1, d))


def _cast_weight_once(w_ref, wb_ref):
    @pl.when(pl.program_id(1) == 0)
    def _():
        wb_ref[...] = w_ref[...].astype(wb_ref.dtype)


def _proj_kernel(*refs, norm, rope, scale, hd, half):
    it = iter(refs)
    a_ref, w_ref = next(it), next(it)
    g_ref = next(it) if norm else None
    c_ref, s_ref = (next(it), next(it)) if rope else (None, None)
    rest = list(it)
    outs, wb_ref = rest[:-1], rest[-1]
    _cast_weight_once(w_ref.at[0], wb_ref)
    if not (norm or rope):
        acc = lax.dot_general(a_ref[...], wb_ref[...], _NT, preferred_element_type=F32)
        if scale is not None:
            acc = acc * scale
        for o in outs:
            o[...] = acc.astype(o.dtype)
        return
    acc = lax.dot_general(a_ref[...], wb_ref[...], _NT, preferred_element_type=F32)
    tm, tn = acc.shape
    first = lax.broadcasted_iota(I32, (tm, hd), 1) < half
    for h in range(tn // hd):
        xh = acc[:, h * hd:(h + 1) * hd]
        if norm:
            ms = jnp.mean(xh * xh, axis=-1, keepdims=True)
            xh = xh * lax.rsqrt(ms + EPS) * g_ref[...]
        if rope:
            partner = jnp.where(first, pltpu.roll(xh, hd - half, 1), pltpu.roll(xh, half, 1))
            xh = xh * c_ref[...] + partner * s_ref[...]
        for o in outs:
            o[:, h * hd:(h + 1) * hd] = xh.astype(o.dtype)


def _project(a, wt, layer, col_off, n, tn, out_dtypes, *, rows=None, gain=None, rope_tabs=None,
             scale=None, hd=LANES):
    k = a.shape[1]
    row0, m = rows if rows is not None else (0, a.shape[0])
    tm = _row_tile(math.gcd(m, row0) if row0 else m, 1088)
    ro = row0 // tm
    assert col_off % SUBLANES == 0 and n % tn == 0
    in_specs = [pl.BlockSpec((tm, k), lambda j, i: (ro + i, 0)),
                pl.BlockSpec((pl.Element(1), pl.Element(tn), pl.Element(k)),
                             lambda j, i: (layer, pl.multiple_of(col_off + j * tn, SUBLANES), 0))]
    args = [a, wt]
    if gain is not None:
        in_specs.append(pl.BlockSpec((1, hd), lambda j, i: (0, 0)))
        args.append(gain.reshape(1, hd))
    if rope_tabs is not None:
        in_specs += [pl.BlockSpec((tm, hd), lambda j, i: (ro + i, 0))] * 2
        args += list(rope_tabs)
    kern = functools.partial(_proj_kernel, norm=gain is not None, rope=rope_tabs is not None,
                             scale=scale, hd=hd, half=hd // 8)
    outs = pl.pallas_call(
        kern,
        grid=(n // tn, m // tm),
        in_specs=in_specs,
        out_specs=[pl.BlockSpec((tm, tn), lambda j, i: (i, j)) for _ in out_dtypes],
        out_shape=[jax.ShapeDtypeStruct((m, n), dt) for dt in out_dtypes],
        scratch_shapes=[pltpu.VMEM((tn, k), BF16)],
        compiler_params=_params(2),
        name="in_proj",
    )(*args)
    return outs


def _select_to_bias(key_ref, bias_ref, lim_ref, nblk, tks, topk, idx_bits, key_valid_min, axis):
    nq = key_ref.shape[1 - axis]
    static = isinstance(nblk, int)
    kf = float(topk)
    blk_shape = (nq, tks) if axis == 1 else (tks, nq)

    def loop(body, init):
        if static:
            c = init
            for j in range(nblk):
                c = body(j, c)
            return c
        return lax.fori_loop(0, nblk, body, init)

    def at(j):
        s = pl.ds(j * tks if static else pl.multiple_of(j * tks, tks), tks)
        return (slice(None), s) if axis == 1 else (s, slice(None))

    def idx(j):
        return j * tks + lax.broadcasted_iota(I32, blk_shape, axis)

    def fold(x):
        if axis == 1:
            p = x[:, :LANES]
            for c in range(1, tks // LANES):
                p = p + x[:, c * LANES:(c + 1) * LANES]
            return p
        return jnp.sum(x.reshape(tks // SUBLANES, SUBLANES, nq), axis=0)

    part_shape = (nq, LANES) if axis == 1 else (SUBLANES, nq)
    vec_shape = (nq, 1) if axis == 1 else (1, nq)

    def count(pred):
        def body(j, c):
            return c + fold(jnp.where(pred(key_ref[at(j)], j), jnp.float32(1), jnp.float32(0)))
        return jnp.sum(loop(body, jnp.zeros(part_shape, F32)), axis=axis, keepdims=True)

    def value_bit(it, thr):
        cand = thr + (jnp.int32(1) << (31 - it))
        return jnp.where(count(lambda kb, j: kb >= cand) >= kf, cand, thr)

    thr = lax.fori_loop(0, 32, value_bit, jnp.full(vec_shape, INT_MIN, I32))
    need = kf - count(lambda kb, j: kb > thr)
    n_ge = count(lambda kb, j: kb >= thr)

    lim_ref[...] = jnp.full(lim_ref.shape, 2 ** idx_bits, I32)

    @pl.when(jnp.max(n_ge) > kf)
    def _():
        def index_bit(it, lim):
            cand = lim + (jnp.int32(1) << (idx_bits - 1 - it))
            n = count(lambda kb, j: jnp.where(kb == thr, idx(j), 2 ** idx_bits) < cand)
            return jnp.where(n <= need, cand, lim)
        lim = lax.fori_loop(0, idx_bits, index_bit, jnp.zeros(vec_shape, I32))
        lim_ref[...] = jnp.broadcast_to(lim, lim_ref.shape)

    lim = lim_ref[:, :1] if axis == 1 else lim_ref[:1, :]

    def write(j, c):
        kb = key_ref[at(j)]
        ok = jnp.where(kb > key_valid_min, 0.0, NEG)
        tie = jnp.where(idx(j) < lim, ok, NEG)
        bias_ref[at(j)] = jnp.where(kb > thr, ok, jnp.where(kb == thr, tie, NEG))
        return c
    loop(write, 0)


def _indexer_scores(iq_ref, iw, ikb, nh, dk):
    acc = None
    for h in range(nh):
        d = lax.dot_general(iq_ref[:, h * dk:(h + 1) * dk], ikb, _NT, preferred_element_type=F32)
        term = jnp.maximum(d, 0.0) * iw[:, h:h + 1]
        acc = term if acc is None else acc + term
    return acc


def _select_prompt_kernel(iq_ref, iwt_ref, ik_ref, bias_ref, key_ref, lim_ref, *,
                          tq, tk, nh, dk, topk, nblk_total, idx_bits, key_valid_min, chunk_shift):
    qi = pl.program_id(1)
    nkv = ((qi + 1) * tq + tk - 1) // tk

    def score_block(j, c):
        off = pl.multiple_of(j * tk, tk)
        ikb = ik_ref[pl.ds(off, tk), :]
        acc = None
        for h in range(nh):
            d = lax.dot_general(ikb, iq_ref[:, h * dk:(h + 1) * dk], _NT, preferred_element_type=F32)
            term = jnp.maximum(d, 0.0) * iwt_ref[h:h + 1, :]
            acc = term if acc is None else acc + term
        kpos = off + lax.broadcasted_iota(I32, (tk, tq), 0)
        qpos = qi * tq + lax.broadcasted_iota(I32, (tk, tq), 1)
        allowed = (kpos >> chunk_shift) <= (qpos >> chunk_shift)
        key_ref[pl.ds(off, tk), :] = _sort_key(jnp.where(allowed, acc, NEG))
        return c
    lax.fori_loop(0, nkv, score_block, 0)

    bias = bias_ref.at[0]
    _select_to_bias(key_ref, bias, lim_ref, nkv, tk, topk, idx_bits, key_valid_min, axis=0)

    def fill(j, c):
        bias[pl.ds(pl.multiple_of(j * tk, tk), tk), :] = jnp.full((tk, tq), NEG, F32)
        return c
    lax.fori_loop(nkv, nblk_total, fill, 0)


def _select_prompt(iq, iwt, ik, nb, t, tq, tk, topk, chunk_shift):
    nh = iwt.shape[0]
    dk = ik.shape[1]
    nq = t // tq
    kern = functools.partial(
        _select_prompt_kernel, tq=tq, tk=tk, nh=nh, dk=dk, topk=topk, nblk_total=t // tk,
        idx_bits=int(t).bit_length(), key_valid_min=_np_sort_key(0.5 * NEG), chunk_shift=chunk_shift)
    return pl.pallas_call(
        kern,
        grid=(nb, nq),
        in_specs=[pl.BlockSpec((tq, nh * dk), lambda b, i: (b * nq + i, 0)),
                  pl.BlockSpec((nh, tq), lambda b, i: (0, b * nq + i)),
                  pl.BlockSpec((t, dk), lambda b, i: (b, 0))],
        out_specs=pl.BlockSpec((1, t, tq), lambda b, i: (b * nq + i, 0, 0)),
        out_shape=jax.ShapeDtypeStruct((nb * nq, t, tq), F32),
        scratch_shapes=[pltpu.VMEM((t, tq), I32), pltpu.VMEM((SUBLANES, tq), I32)],
        compiler_params=_params(2),
        name="select_prompt",
    )(iq, iwt, ik)


def _select_sample_kernel(iq_ref, iw_ref, ikc_ref, ikn_ref, bias_ref, key_ref, m_ref, *,
                          ts, past, sw, nh, dk, topk, idx_bits, key_valid_min, chunk_shift):
    iw = iw_ref[...]
    for j in range(past // sw):
        ikb = ikc_ref[j * sw:(j + 1) * sw, :].astype(BF16)
        acc = _indexer_scores(iq_ref, iw, ikb, nh, dk)
        qpos = past + lax.broadcasted_iota(I32, (ts, sw), 0)
        kpos = j * sw + lax.broadcasted_iota(I32, (ts, sw), 1)
        allowed = (kpos >> chunk_shift) <= (qpos >> chunk_shift)
        key_ref[:, j * sw:(j + 1) * sw] = _sort_key(jnp.where(allowed, acc, NEG))
    key_ref[:, past:past + LANES] = jnp.full((ts, LANES), _np_sort_key(NEG), I32)
    acc = _indexer_scores(iq_ref, iw, ikn_ref[...], nh, dk)
    qpos = past + lax.broadcasted_iota(I32, (ts, ts), 0)
    kpos = past + lax.broadcasted_iota(I32, (ts, ts), 1)
    allowed = (kpos >> chunk_shift) <= (qpos >> chunk_shift)
    key_ref[:, past:past + ts] = _sort_key(jnp.where(allowed, acc, NEG))
    _select_to_bias(key_ref, bias_ref, m_ref, past // LANES + 1, LANES, topk, idx_bits, key_valid_min,
                    axis=1)


def _select_sample(iq, iw, ik_cache, ik_new, row_blk_off, layer, nb, ts, topk, chunk_shift):
    nh = iw.shape[1]
    past, dk = ik_cache.shape[2], ik_cache.shape[3]
    assert past % LANES == 0 and ts <= LANES
    s_pad = past + LANES
    kern = functools.partial(
        _select_sample_kernel, ts=ts, past=past, sw=_col_tile((past,), 512), nh=nh, dk=dk, topk=topk,
        idx_bits=int(s_pad).bit_length(), key_valid_min=_np_sort_key(0.5 * NEG), chunk_shift=chunk_shift)
    return pl.pallas_call(
        kern,
        grid=(nb,),
        in_specs=[pl.BlockSpec((ts, nh * dk), lambda b: (row_blk_off + b, 0)),
                  pl.BlockSpec((ts, nh), lambda b: (row_blk_off + b, 0)),
                  pl.BlockSpec((None, None, past, dk), lambda b: (layer, b, 0, 0)),
                  pl.BlockSpec((ts, dk), lambda b: (b, 0))],
        out_specs=pl.BlockSpec((ts, s_pad), lambda b: (b, 0)),
        out_shape=jax.ShapeDtypeStruct((nb * ts, s_pad), F32),
        scratch_shapes=[pltpu.VMEM((ts, s_pad), I32), pltpu.VMEM((ts, LANES), I32)],
        compiler_params=_params(1),
        name="select_sample",
    )(iq, iw, ik_cache, ik_new)


def _attn_prompt_kernel(q_ref, k_ref, vt_ref, bias_ref, o_ref, m_ref, l_ref, acc_ref, s_ref, mc_ref, *,
                        tq, tk, grp, hd, scale_log2e):
    nkv = ((pl.program_id(1) + 1) * tq + tk - 1) // tk
    m_ref[...] = jnp.full(m_ref.shape, -jnp.inf, F32)
    l_ref[...] = jnp.zeros(l_ref.shape, F32)
    acc_ref[...] = jnp.zeros(acc_ref.shape, F32)

    def body(j, c):
        off = pl.multiple_of(j * tk, tk)
        kb = k_ref[pl.ds(off, tk), :]
        vtb = vt_ref[:, pl.ds(off, tk)]
        bias = bias_ref[0, pl.ds(off, tk), :]
        for hh in range(grp):
            qh = q_ref[:, hh * hd:(hh + 1) * hd]
            s = lax.dot_general(kb, qh, _NT, preferred_element_type=F32) * scale_log2e + bias
            s_ref[hh] = s
            mc_ref[hh] = jnp.max(s, axis=0, keepdims=True)
        for hh in range(grp):
            s = s_ref[hh]
            m = m_ref[hh]
            m_new = jnp.maximum(m, mc_ref[hh])
            alpha = jnp.exp2(m - m_new)
            p = jnp.exp2(s - m_new)
            l_ref[hh] = alpha * l_ref[hh] + jnp.sum(p, axis=0, keepdims=True)
            acc_ref[hh] = alpha * acc_ref[hh] + jnp.dot(vtb, p.astype(BF16),
                                                        preferred_element_type=F32)
            m_ref[hh] = m_new
        return c
    lax.fori_loop(0, nkv, body, 0)

    for hh in range(grp):
        o_ref[:, hh * hd:(hh + 1) * hd] = (acc_ref[hh] / l_ref[hh]).T.astype(o_ref.dtype)


def _attn_prompt(q, k, vt, bias_t, nb, t, tq, tk, kvh, hd):
    aw = q.shape[1]
    grp = aw // (kvh * hd)
    nq = t // tq
    kern = functools.partial(_attn_prompt_kernel, tq=tq, tk=tk, grp=grp, hd=hd,
                             scale_log2e=hd ** -0.5 * math.log2(math.e))
    return pl.pallas_call(
        kern,
        grid=(nb, nq, kvh),
        in_specs=[pl.BlockSpec((tq, grp * hd), lambda b, i, g: (b * nq + i, g)),
                  pl.BlockSpec((t, hd), lambda b, i, g: (b, g)),
                  pl.BlockSpec((hd, t), lambda b, i, g: (g, b)),
                  pl.BlockSpec((1, t, tq), lambda b, i, g: (b * nq + i, 0, 0))],
        out_specs=pl.BlockSpec((tq, grp * hd), lambda b, i, g: (b * nq + i, g)),
        out_shape=jax.ShapeDtypeStruct((nb * t, aw), BF16),
        scratch_shapes=[pltpu.VMEM((grp, 1, tq), F32),
                        pltpu.VMEM((grp, 1, tq), F32),
                        pltpu.VMEM((grp, hd, tq), F32),
                        pltpu.VMEM((grp, tk, tq), F32),
                        pltpu.VMEM((grp, 1, tq), F32)],
        compiler_params=_params(3),
        name="attn_prompt",
    )(q, k, vt, bias_t)


def _attn_sample_kernel(q_ref, kc_ref, vc_ref, kn_ref, vn_ref, bias_ref, o_ref, *,
                        ts, past, kvh, grp, hd, scale):
    bias_c = bias_ref[:, :past]
    bias_n = bias_ref[:, past:past + ts]
    for g in range(kvh):
        kc = kc_ref[pl.ds(g, past, stride=kvh), :].astype(BF16)
        vc = vc_ref[pl.ds(g, past, stride=kvh), :].astype(BF16)
        kn = kn_ref[:, g * hd:(g + 1) * hd]
        vn = vn_ref[:, g * hd:(g + 1) * hd]
        for hh in range(g * grp, (g + 1) * grp):
            qh = q_ref[:, hh * hd:(hh + 1) * hd]
            sc = lax.dot_general(qh, kc, _NT, preferred_element_type=F32) * scale + bias_c
            sn = lax.dot_general(qh, kn, _NT, preferred_element_type=F32) * scale + bias_n
            m = jnp.maximum(jnp.max(sc, axis=-1, keepdims=True), jnp.max(sn, axis=-1, keepdims=True))
            pc = jnp.exp(sc - m)
            pn = jnp.exp(sn - m)
            l = jnp.sum(pc, axis=-1, keepdims=True) + jnp.sum(pn, axis=-1, keepdims=True)
            acc = (jnp.dot(pc.astype(BF16), vc, preferred_element_type=F32)
                   + jnp.dot(pn.astype(BF16), vn, preferred_element_type=F32))
            o_ref[:, hh * hd:(hh + 1) * hd] = (acc / l).astype(o_ref.dtype)


def _attn_sample(q, k_cache, v_cache, k_new, v_new, bias, row_blk_off, layer, nb, ts, kvh, hd):
    aw = q.shape[1]
    grp = aw // (kvh * hd)
    past = k_cache.shape[2]
    depth = k_cache.shape[0]
    k_cache = k_cache.reshape(depth, nb, past * kvh, hd)
    v_cache = v_cache.reshape(depth, nb, past * kvh, hd)
    cache_spec = lambda: pl.BlockSpec((None, None, past * kvh, hd), lambda b: (layer, b, 0, 0))
    kern = functools.partial(_attn_sample_kernel, ts=ts, past=past, kvh=kvh, grp=grp, hd=hd,
                             scale=hd ** -0.5)
    return pl.pallas_call(
        kern,
        grid=(nb,),
        in_specs=[pl.BlockSpec((ts, aw), lambda b: (row_blk_off + b, 0)),
                  cache_spec(), cache_spec(),
                  pl.BlockSpec((ts, kvh * hd), lambda b: (b, 0)),
                  pl.BlockSpec((ts, kvh * hd), lambda b: (b, 0)),
                  pl.BlockSpec((ts, past + LANES), lambda b: (b, 0))],
        out_specs=pl.BlockSpec((ts, aw), lambda b: (b, 0)),
        out_shape=jax.ShapeDtypeStruct((nb * ts, aw), BF16),
        compiler_params=_params(1),
        name="attn_sample",
    )(q, k_cache, v_cache, k_new, v_new, bias)


def _lru_kernel(xl_ref, yl_ref, c0_ref, h0_ref, cw_ref, cb_ref, wa_ref, ba_ref, wx_ref, bx_ref,
                lam_ref, o_ref, hl_ref, cn_ref, xbuf, hc, a_s, b_s, h_s, *, tt, cw, bw):
    ti = pl.program_id(2)
    pad = SUBLANES
    tail = CONV_W - 1

    @pl.when(ti == 0)
    def _():
        xbuf[pad - tail:pad, :] = c0_ref[0]
        hc[...] = jnp.broadcast_to(h0_ref[0], hc.shape)
        a_s[:pad, :] = jnp.ones((pad, cw), F32)
        b_s[:pad, :] = jnp.zeros((pad, cw), F32)

    x = xl_ref[...]
    xbuf[pad:, :] = x
    u = cb_ref[...]
    for j in range(CONV_W):
        u = u + xbuf[pad - tail + j:pad - tail + j + tt, :] * cw_ref[j:j + 1, :]
    xbuf[pad - tail:pad, :] = xbuf[pad + tt - tail:pad + tt, :]

    @pl.when(ti == pl.num_programs(2) - 1)
    def _():
        cn_ref[0] = x[tt - tail:, :]

    ub = u.astype(BF16)
    for n in range(cw // bw):
        cs = slice(n * bw, (n + 1) * bw)
        un = ub[:, cs]
        r = jax.nn.sigmoid(jnp.dot(un, wa_ref[n], preferred_element_type=F32) + ba_ref[:, cs])
        i = jax.nn.sigmoid(jnp.dot(un, wx_ref[n], preferred_element_type=F32) + bx_ref[:, cs])
        log_a = -LRU_C * r * jax.nn.softplus(-lam_ref[:, cs])
        a_s[pad:, cs] = jnp.exp(log_a)
        t = jnp.tanh(log_a)
        b_s[pad:, cs] = jnp.sqrt(-2.0 * t / (1.0 - t)) * (i * u[:, cs])

    ngrp = tt // SUBLANES
    row = lax.broadcasted_iota(I32, (ngrp, SUBLANES, cw), 1)
    a = a_s[pad:, :].reshape(ngrp, SUBLANES, cw)
    b = b_s[pad:, :].reshape(ngrp, SUBLANES, cw)
    d = 1
    while d < SUBLANES:
        keep = row >= d
        a_sh = jnp.where(keep, pltpu.roll(a, d, 1), 1.0)
        b_sh = jnp.where(keep, pltpu.roll(b, d, 1), 0.0)
        b = a * b_sh + b
        a = a * a_sh
        d *= 2
    a_s[pad:, :] = a.reshape(tt, cw)
    b_s[pad:, :] = b.reshape(tt, cw)

    carry = hc[...]
    for g in range(tt // SUBLANES):
        rs = slice(pad + g * SUBLANES, pad + (g + 1) * SUBLANES)
        h = a_s[rs, :] * carry + b_s[rs, :]
        h_s[g * SUBLANES:(g + 1) * SUBLANES, :] = h
        carry = jnp.broadcast_to(h[SUBLANES - 1:, :], hc.shape)
    hc[...] = carry

    @pl.when(ti == pl.num_programs(2) - 1)
    def _():
        hl_ref[0] = carry[:1, :]

    o_ref[...] = (h_s[...] * jax.nn.gelu(yl_ref[...])).astype(o_ref.dtype)


def _lru_branch(tail_proj, xl_col, yl_col, row_off, conv0, h0, p, nb, t):
    r = conv0.shape[-1]
    bw = p['lru_wa'].shape[-1]
    cw = bw * max(1, min(r // bw, 512 // bw))
    tt = t if t <= 256 else 256
    assert t % tt == 0 and r % cw == 0 and row_off % tt == 0 and tt % SUBLANES == 0
    nt = t // tt
    ro = row_off // tt
    xo, yo = xl_col // cw, yl_col // cw
    assert xl_col % cw == 0 and yl_col % cw == 0
    nblk = cw // bw
    kern = functools.partial(_lru_kernel, tt=tt, cw=cw, bw=bw)
    vec = lambda: pl.BlockSpec((1, cw), lambda b, c, i: (0, c))
    return pl.pallas_call(
        kern,
        grid=(nb, r // cw, nt),
        in_specs=[pl.BlockSpec((tt, cw), lambda b, c, i: (ro + b * nt + i, xo + c)),
                  pl.BlockSpec((tt, cw), lambda b, c, i: (ro + b * nt + i, yo + c)),
                  pl.BlockSpec((1, CONV_W - 1, cw), lambda b, c, i: (b, 0, c)),
                  pl.BlockSpec((1, 1, cw), lambda b, c, i: (b, 0, c)),
                  pl.BlockSpec((CONV_W, cw), lambda b, c, i: (0, c)),
                  vec(),
                  pl.BlockSpec((nblk, bw, bw), lambda b, c, i: (c, 0, 0)),
                  vec(),
                  pl.BlockSpec((nblk, bw, bw), lambda b, c, i: (c, 0, 0)),
                  vec(), vec()],
        out_specs=[pl.BlockSpec((tt, cw), lambda b, c, i: (b * nt + i, c)),
                   pl.BlockSpec((1, 1, cw), lambda b, c, i: (b, 0, c)),
                   pl.BlockSpec((1, CONV_W - 1, cw), lambda b, c, i: (b, 0, c))],
        out_shape=[jax.ShapeDtypeStruct((nb * t, r), BF16),
                   jax.ShapeDtypeStruct((nb, 1, r), F32),
                   jax.ShapeDtypeStruct((nb, CONV_W - 1, r), F32)],
        scratch_shapes=[pltpu.VMEM((tt + SUBLANES, cw), F32),
                        pltpu.VMEM((SUBLANES, cw), F32),
                        pltpu.VMEM((tt + SUBLANES, cw), F32),
                        pltpu.VMEM((tt + SUBLANES, cw), F32),
                        pltpu.VMEM((tt, cw), F32)],
        compiler_params=_params(3),
        name="conv_rglru",
    )(tail_proj, tail_proj, conv0, h0.reshape(nb, 1, r), p['conv_w'], p['conv_b'].reshape(1, r),
      p['wa_bf'], p['lru_ba'].reshape(1, r), p['wx_bf'], p['lru_bx'].reshape(1, r),
      p['lru_lambda'].reshape(1, r))


def _mix_kernel(oa_ref, ol_ref, w_ref, ga_ref, gl_ref, o_ref, w1b_ref, w2b_ref, *, ka):
    @pl.when(pl.program_id(1) == 0)
    def _():
        w1b_ref[...] = w_ref[:ka, :].astype(BF16)
        w2b_ref[...] = w_ref[ka:, :].astype(BF16)
    ya = jnp.dot(oa_ref[...], w1b_ref[...], preferred_element_type=F32)
    yl = jnp.dot(ol_ref[...], w2b_ref[...], preferred_element_type=F32)
    mixed = jax.nn.sigmoid(ga_ref[...]) * ya + jax.nn.sigmoid(gl_ref[...]) * yl
    o_ref[...] = mixed.astype(o_ref.dtype)


def _mix(oa, ol, w, layer, tail_proj, row0, ga_col, gl_col):
    m, ka = oa.shape
    kl = ol.shape[1]
    n = w.shape[2]
    tm = _row_tile(math.gcd(m, row0) if row0 else m, 512)
    ro = row0 // tm
    tn = _col_tile((n, ga_col, gl_col), 512)
    go, lo = ga_col // tn, gl_col // tn
    return pl.pallas_call(
        functools.partial(_mix_kernel, ka=ka),
        grid=(n // tn, m // tm),
        in_specs=[pl.BlockSpec((tm, ka), lambda j, i: (i, 0)),
                  pl.BlockSpec((tm, kl), lambda j, i: (i, 0)),
                  pl.BlockSpec((None, ka + kl, tn), lambda j, i: (layer, 0, j),
                               pipeline_mode=pl.Buffered(1)),
                  pl.BlockSpec((tm, tn), lambda j, i: (ro + i, go + j)),
                  pl.BlockSpec((tm, tn), lambda j, i: (ro + i, lo + j))],
        out_specs=pl.BlockSpec((tm, tn), lambda j, i: (i, j)),
        out_shape=jax.ShapeDtypeStruct((m, n), BF16),
        scratch_shapes=[pltpu.VMEM((ka, tn), BF16), pltpu.VMEM((kl, tn), BF16)],
        compiler_params=_params(2),
        name="branch_mix",
    )(oa, ol, w, tail_proj, tail_proj)


def _out_proj_kernel(a_ref, w_ref, x_ref, o_ref, wb_ref):
    _cast_weight_once(w_ref, wb_ref)
    o_ref[...] = x_ref[...] + jnp.dot(a_ref[...], wb_ref[...], preferred_element_type=F32)


def _out_proj(a, w, layer, x):
    m, k = a.shape
    n = w.shape[2]
    tm = _row_tile(m, 1024)
    tn = _col_tile((n,), 512)
    return pl.pallas_call(
        _out_proj_kernel,
        grid=(n // tn, m // tm),
        in_specs=[pl.BlockSpec((tm, k), lambda j, i: (i, 0)),
                  pl.BlockSpec((None, k, tn), lambda j, i: (layer, 0, j)),
                  pl.BlockSpec((tm, tn), lambda j, i: (i, j))],
        out_specs=pl.BlockSpec((tm, tn), lambda j, i: (i, j)),
        out_shape=jax.ShapeDtypeStruct((m, n), F32),
        scratch_shapes=[pltpu.VMEM((k, tn), BF16)],
        compiler_params=_params(2),
        name="out_proj",
    )(a, w, x)


def _residual_mm_kernel(a_ref, w_ref, x_ref, o_ref):
    o_ref[...] = x_ref[...] + jnp.dot(a_ref[...], w_ref[...], preferred_element_type=F32)


def _residual_mm(a, row0, w, x, tm_target, tn_target):
    k = a.shape[1]
    m = x.shape[0]
    n = w.shape[1]
    tm = _row_tile(math.gcd(m, row0) if row0 else m, tm_target)
    ro = row0 // tm
    tn = _col_tile((n,), tn_target)
    return pl.pallas_call(
        _residual_mm_kernel,
        grid=(m // tm, n // tn),
        in_specs=[pl.BlockSpec((tm, k), lambda i, j: (ro + i, 0)),
                  pl.BlockSpec((k, tn), lambda i, j: (0, j)),
                  pl.BlockSpec((tm, tn), lambda i, j: (i, j))],
        out_specs=pl.BlockSpec((tm, tn), lambda i, j: (i, j)),
        out_shape=jax.ShapeDtypeStruct((m, n), F32),
        compiler_params=_params(2),
        name="residual_matmul",
    )(a, w, x)


def _swiglu_kernel(a_ref, wg_ref, wu_ref, o_ref, wgb_ref, wub_ref):
    _cast_weight_once(wg_ref, wgb_ref)
    _cast_weight_once(wu_ref, wub_ref)
    a = a_ref[...]
    g = jnp.dot(a, wgb_ref[...], preferred_element_type=F32)
    u = jnp.dot(a, wub_ref[...], preferred_element_type=F32)
    o_ref[...] = (jax.nn.silu(g) * u).astype(o_ref.dtype)


def _swiglu(a, wg, wu, layer):
    m, k = a.shape
    n = wg.shape[2]
    tm = _row_tile(m, 1088)
    tn = _col_tile((n,), 256)
    wspec = lambda: pl.BlockSpec((None, k, tn), lambda j, i: (layer, 0, j))
    return pl.pallas_call(
        _swiglu_kernel,
        grid=(n // tn, m // tm),
        in_specs=[pl.BlockSpec((tm, k), lambda j, i: (i, 0)), wspec(), wspec()],
        out_specs=pl.BlockSpec((tm, tn), lambda j, i: (i, j)),
        out_shape=jax.ShapeDtypeStruct((m, n), BF16),
        scratch_shapes=[pltpu.VMEM((k, tn), BF16), pltpu.VMEM((k, tn), BF16)],
        compiler_params=_params(2),
        name="swiglu",
    )(a, wg, wu)


def _rope_tables(pos, hd):
    rot = hd // 4
    half = rot // 2
    inv = ROPE_THETA ** (-jnp.arange(half, dtype=F32) * 2.0 / rot)
    ang = pos.astype(F32)[:, None] * inv[None, :]
    cos, sin = jnp.cos(ang), jnp.sin(ang)
    n = pos.shape[0]
    c = jnp.concatenate([cos, cos, jnp.ones((n, hd - rot), F32)], axis=-1)
    s = jnp.concatenate([-sin, sin, jnp.zeros((n, hd - rot), F32)], axis=-1)
    return c, s


def _layer(xp, xs, dims, layer, caches, states, p):
    bp, tp, bs, ts, past, kvh, hd, di, ih = dims
    cache_k, cache_v, cache_ik = caches
    state_lru, state_conv = states
    d = xp.shape[1]
    mp, ms = bp * tp, bs * ts
    r = state_lru.shape[-1]
    aw = p['w_branch'].shape[1] - r
    kvw = kvh * hd
    assert hd == LANES and di == LANES and mp % ts == 0
    chunk_shift = CHUNK.bit_length() - 1
    assert 1 << chunk_shift == CHUNK
    rows_p, rows_s = (0, mp), (mp, ms)

    pos = jnp.concatenate([jnp.tile(jnp.arange(tp, dtype=I32), bp),
                           jnp.tile(past + jnp.arange(ts, dtype=I32), bs)])
    tabs = _rope_tables(pos, hd)

    xn = _rmsnorm(xp, xs, p['norm_mix'])
    tn = _col_tile((aw, kvw, ih * di), 512)
    c_k, c_v, c_iq = aw, aw + kvw, aw + 2 * kvw
    w_in = p['w_in_t']
    c_iw = c_iq + ih * di
    c_ik = c_iw + ih
    c_xl = c_ik + di
    (q_bf,) = _project(xn, w_in, layer, 0, aw, tn, (BF16,), gain=p['norm_q'], rope_tabs=tabs)
    (iq_bf,) = _project(xn, w_in, layer, c_iq, ih * di, tn, (BF16,), rope_tabs=tabs)
    (iw,) = _project(xn, w_in, layer, c_iw, ih, ih, (F32,), scale=float((ih * di) ** -0.5))
    tn_tail = _col_tile((r, d), 512)
    (tail,) = _project(xn, w_in, layer, c_xl, 2 * r + 2 * d, tn_tail, (F32,))
    xl_col, yl_col, ga_col, gl_col = 0, r, 2 * r, 2 * r + d
    kvi = []
    for rows in (rows_p, rows_s):
        k_f, k_bf = _project(xn, w_in, layer, c_k, kvw, tn, (F32, BF16), rows=rows,
                             gain=p['norm_k'], rope_tabs=tabs)
        v_f, v_bf = _project(xn, w_in, layer, c_v, kvw, tn, (F32, BF16), rows=rows)
        ik_f, ik_bf = _project(xn, w_in, layer, c_ik, di, di, (F32, BF16), rows=rows,
                               gain=p['norm_idx_k'], rope_tabs=tabs)
        kvi.append((k_f, k_bf, v_f, v_bf, ik_f, ik_bf))
    (kp_f, kp_bf, vp_f, vp_bf, ikp_f, ikp_bf), (ks_f, ks_bf, vs_f, vs_bf, iks_f, iks_bf) = kvi

    tq = 512 if tp % 512 == 0 else (256 if tp % 256 == 0 else LANES)
    tk = 512 if tp % 512 == 0 else tq
    bias_p = _select_prompt(iq_bf, iw.T, ikp_bf, bp, tp, tq, tk, min(TOPK_MAX, tp // 4), chunk_shift)
    oa_p = _attn_prompt(q_bf, kp_bf, vp_bf.T, bias_p, bp, tp, tq, tk, kvh, hd)
    sblk = mp // ts
    bias_s = _select_sample(iq_bf, iw, cache_ik, iks_bf, sblk, layer, bs, ts,
                            min(TOPK_MAX, (past + ts) // 4), chunk_shift)
    oa_s = _attn_sample(q_bf, cache_k, cache_v, ks_bf, vs_bf, bias_s, sblk, layer, bs, ts, kvh, hd)

    ol_p, h_p, c_p = _lru_branch(tail, xl_col, yl_col, 0, jnp.zeros((bp, CONV_W - 1, r), F32),
                                 jnp.zeros((bp, r), F32), p, bp, tp)
    ol_s, h_s, c_s = _lru_branch(tail, xl_col, yl_col, mp, state_conv, state_lru, p, bs, ts)

    x1 = []
    for oa, ol, x, row0 in ((oa_p, ol_p, xp, 0), (oa_s, ol_s, xs, mp)):
        mixed = _mix(oa, ol, p['w_branch'], layer, tail, row0, ga_col, gl_col)
        x1.append(_out_proj(mixed, p['w_out'], layer, x))
    xn2 = _rmsnorm(x1[0], x1[1], p['norm_ffn'])
    hff = _swiglu(xn2, p['w_gate'], p['w_up'], layer)
    yp = _residual_mm(hff, 0, p['w_down_bf'], x1[0], 512, 256)
    ys = _residual_mm(hff, mp, p['w_down_bf'], x1[1], 512, 256)

    new_p = (kp_f.reshape(bp, tp, kvh, hd), vp_f.reshape(bp, tp, kvh, hd),
             ikp_f.reshape(bp, tp, di), h_p.reshape(bp, r), c_p)
    new_s = (ks_f.reshape(bs, ts, kvh, hd), vs_f.reshape(bs, ts, kvh, hd),
             iks_f.reshape(bs, ts, di), h_s.reshape(bs, r), c_s)
    return yp, ys, new_p, new_s


def kernel(x_prompt, x_sample, cache_k, cache_v, cache_idx_k, state_lru, state_conv, norm_mix, w_in,
           norm_q, norm_k, norm_idx_k, conv_w, conv_b, lru_wa, lru_ba, lru_wx, lru_bx, lru_lambda,
           w_branch, w_out, norm_ffn, w_gate, w_up, w_down):
    depth = w_in.shape[0]
    bp, tp, d = x_prompt.shape
    bs, ts, _ = x_sample.shape
    past, kvh, hd = cache_k.shape[2], cache_k.shape[3], cache_k.shape[4]
    di = cache_idx_k.shape[-1]
    r = state_lru.shape[-1]
    aw = w_branch.shape[1] - r
    kvw = kvh * hd
    ih = (w_in.shape[-1] - aw - 2 * kvw - di - 2 * r - 2 * d) // (di + 1)
    dims =(bp, tp, bs, ts, past, kvh, hd, di, ih)
    xp, xs = x_prompt.reshape(bp * tp, d), x_sample.reshape(bs * ts, d)
    caches = (cache_k, cache_v, cache_idx_k)
    news_p, news_s = [], []
    for l in range(depth):
        p = {
            'norm_mix': norm_mix[l], 'norm_q': norm_q[l], 'norm_k': norm_k[l],
            'norm_idx_k': norm_idx_k[l], 'norm_ffn': norm_ffn[l],
            'w_in_t': jnp.swapaxes(w_in, 1, 2),
            'conv_w': conv_w[l], 'conv_b': conv_b[l],
            'lru_wa': lru_wa[l], 'wa_bf': lru_wa[l].astype(BF16), 'wx_bf': lru_wx[l].astype(BF16),
            'lru_ba': lru_ba[l], 'lru_bx': lru_bx[l], 'lru_lambda': lru_lambda[l],
            'w_branch': w_branch, 'w_out': w_out, 'w_gate': w_gate, 'w_up': w_up,
            'w_down_bf': w_down[l].astype(BF16),
        }
        xp, xs, new_p, new_s = _layer(xp, xs, dims, l, caches, (state_lru[l], state_conv[l]), p)
        news_p.append(new_p)
        news_s.append(new_s)

    stack = lambda news, i: jnp.stack([n[i] for n in news])
    return (xp.reshape(bp, tp, d), xs.reshape(bs, ts, d),
            stack(news_p, 0), stack(news_p, 1), stack(news_p, 2), stack(news_p, 3), stack(news_p, 4),
            stack(news_s, 0), stack(news_s, 1), stack(news_s, 2), stack(news_s, 3), stack(news_s, 4))
```

```python
import functools
import math

import numpy as np
import jax
import jax.numpy as jnp
from jax import lax
from jax.experimental import pallas as pl
from jax.experimental.pallas import tpu as pltpu

CHUNK = 64
TOPK_MAX = 256
ROPE_THETA = 500000.0
CONV_W = 4
LRU_C = 8.0
EPS = 1e-6
NEG = -1e30

LANES = 128
SUBLANES = 8
BF16_ROWS = 16
V7X_VMEM_LIMIT_BYTES = 56 * 1024 * 1024

F32 = jnp.float32
BF16 = jnp.bfloat16
I32 = jnp.int32
INT_MIN = -(2 ** 31)
_NT = (((1,), (1,)), ((), ()))


def _params(n_grid):
    return pltpu.CompilerParams(
        dimension_semantics=("arbitrary",) * n_grid,
        vmem_limit_bytes=V7X_VMEM_LIMIT_BYTES)


def _row_tile(m, target):
    best = None
    for t in range(BF16_ROWS, min(m, target) + 1, BF16_ROWS):
        if m % t == 0:
            best = t
    assert best is not None, (m, target)
    return best


def _col_tile(sizes, target):
    t = target
    while t >= LANES:
        if all(s % t == 0 for s in sizes):
            return t
        t -= LANES
    raise ValueError(sizes)


def _key_to_float(key):
    return pltpu.bitcast(key ^ ((key >> 31) & jnp.int32(0x7FFFFFFF)), F32)


def _rmsnorm_kernel(xp_ref, xs_ref, g_ref, o_ref, *, n_prompt_blocks):
    def norm(x_ref):
        x = x_ref[...]
        ms = jnp.mean(x * x, axis=-1, keepdims=True)
        o_ref[...] = (x * lax.rsqrt(ms + EPS) * g_ref[...]).astype(o_ref.dtype)

    @pl.when(pl.program_id(0) < n_prompt_blocks)
    def _():
        norm(xp_ref)

    @pl.when(pl.program_id(0) >= n_prompt_blocks)
    def _():
        norm(xs_ref)


def _rmsnorm(xp, xs, g):
    mp, d = xp.shape
    ms = xs.shape[0]
    tm = _row_tile(math.gcd(mp, ms), 256)
    npb = mp // tm
    return pl.pallas_call(
        functools.partial(_rmsnorm_kernel, n_prompt_blocks=npb),
        grid=((mp + ms) // tm,),
        in_specs=[pl.BlockSpec((tm, d), lambda i: (jnp.minimum(i, npb - 1), 0)),
                  pl.BlockSpec((tm, d), lambda i: (jnp.maximum(i - npb, 0), 0)),
                  pl.BlockSpec((1, d), lambda i: (0, 0))],
        out_specs=pl.BlockSpec((tm, d), lambda i: (i, 0)),
        out_shape=jax.ShapeDtypeStruct((mp + ms, d), BF16),
        compiler_params=_params(1),
        name="rmsnorm",
    )(xp, xs, g.reshape(1, d))


def _cast_weight_once(w_ref, wb_ref):
    @pl.when(pl.program_id(1) == 0)
    def _():
        wb_ref[...] = w_ref[...].astype(wb_ref.dtype)


def _proj_kernel(*refs, norm, rope, scale, hd, half):
    it = iter(refs)
    a_ref, w_ref = next(it), next(it)
    g_ref, e_ref = (next(it), next(it)) if norm else (None, None)
    c_ref, s_ref = (next(it), next(it)) if rope else (None, None)
    rest = list(it)
    outs, wb_ref = rest[:-1], rest[-1]
    _cast_weight_once(w_ref.at[0], wb_ref)
    if not (norm or rope):
        acc = lax.dot_general(a_ref[...], wb_ref[...], _NT, preferred_element_type=F32)
        if scale is not None:
            acc = acc * scale
        for o in outs:
            o[...] = acc.astype(o.dtype)
        return
    acc = lax.dot_general(a_ref[...], wb_ref[...], _NT, preferred_element_type=F32)
    tm, tn = acc.shape
    first = lax.broadcasted_iota(I32, (tm, hd), 1) < half
    for h in range(tn // hd):
        xh = acc[:, h * hd:(h + 1) * hd]
        if norm:
            both = jnp.concatenate([xh * xh, xh], axis=1).astype(BF16)
            r = jnp.dot(both, e_ref[...], preferred_element_type=F32)
            rs = lax.rsqrt(r[:, :hd] + EPS)
            partner = r[:, hd:] * rs * g_ref[1:2, :]
            xh = xh * rs * g_ref[0:1, :]
        else:
            partner = jnp.where(first, pltpu.roll(xh, hd - half, 1), pltpu.roll(xh, half, 1))
        xh = xh * c_ref[...] + partner * s_ref[...]
        for o in outs:
            o[:, h * hd:(h + 1) * hd] = xh.astype(o.dtype)


def _project(a, wt, layer, col_off, n, tn, out_dtypes, *, rows=None, gain=None, rope_tabs=None,
             scale=None, hd=LANES):
    k = a.shape[1]
    row0, m = rows if rows is not None else (0, a.shape[0])
    tm = _row_tile(math.gcd(m, row0) if row0 else m, 1088)
    ro = row0 // tm
    assert col_off % SUBLANES == 0 and n % tn == 0
    in_specs = [pl.BlockSpec((tm, k), lambda j, i: (ro + i, 0)),
                pl.BlockSpec((pl.Element(1), pl.Element(tn), pl.Element(k)),
                             lambda j, i: (layer, pl.multiple_of(col_off + j * tn, SUBLANES), 0))]
    args = [a, wt]
    if gain is not None:
        assert rope_tabs is not None
        half = hd // 8
        swap = np.arange(hd)
        swap[:half], swap[half:2 * half] = np.arange(half, 2 * half), np.arange(half)
        e = np.zeros((2 * hd, 2 * hd), np.float32)
        e[:hd, :hd] = 1.0 / hd
        e[hd + swap[:2 * half], hd + np.arange(2 * half)] = 1.0
        in_specs += [pl.BlockSpec((2, hd), lambda j, i: (0, 0)),
                     pl.BlockSpec((2 * hd, 2 * hd), lambda j, i: (0, 0))]
        args += [jnp.stack([gain, gain[swap]]), jnp.asarray(e, BF16)]
    if rope_tabs is not None:
        in_specs += [pl.BlockSpec((tm, hd), lambda j, i: (ro + i, 0))] * 2
        args += list(rope_tabs)
    kern = functools.partial(_proj_kernel, norm=gain is not None, rope=rope_tabs is not None,
                             scale=scale, hd=hd, half=hd // 8)
    outs = pl.pallas_call(
        kern,
        grid=(n // tn, m // tm),
        in_specs=in_specs,
        out_specs=[pl.BlockSpec((tm, tn), lambda j, i: (i, j)) for _ in out_dtypes],
        out_shape=[jax.ShapeDtypeStruct((m, n), dt) for dt in out_dtypes],
        scratch_shapes=[pltpu.VMEM((tn, k), BF16)],
        compiler_params=_params(2),
        name="in_proj",
    )(*args)
    return outs


def _select_to_bias(key_ref, bias_ref, lim_ref, nblk, tks, topk, idx_bits, axis):
    nq = key_ref.shape[1 - axis]
    static = isinstance(nblk, int)
    kf = float(topk)
    blk_shape = (nq, tks) if axis == 1 else (tks, nq)

    def loop(body, init):
        if static:
            c = init
            for j in range(nblk):
                c = body(j, c)
            return c
        return lax.fori_loop(0, nblk, body, init)

    def at(j):
        s = pl.ds(j * tks if static else pl.multiple_of(j * tks, tks), tks)
        return (slice(None), s) if axis == 1 else (s, slice(None))

    def idx(j):
        return j * tks + lax.broadcasted_iota(I32, blk_shape, axis)

    def fold(x):
        if axis == 1:
            p = x[:, :LANES]
            for c in range(1, tks // LANES):
                p = p + x[:, c * LANES:(c + 1) * LANES]
            return p
        return jnp.sum(x.reshape(tks // SUBLANES, SUBLANES, nq), axis=0)

    part_shape = (nq, LANES) if axis == 1 else (SUBLANES, nq)
    vec_shape = (nq, 1) if axis == 1 else (1, nq)

    def count(pred):
        def body(j, c):
            return c + fold(jnp.where(pred(key_ref[at(j)], j), jnp.float32(1), jnp.float32(0)))
        return jnp.sum(loop(body, jnp.zeros(part_shape, F32)), axis=axis, keepdims=True)

    def value_bit(it, thr_key):
        cand = thr_key + (jnp.int32(1) << (31 - it))
        cand_f = _key_to_float(cand)
        return jnp.where(count(lambda kb, j: kb >= cand_f) >= kf, cand, thr_key)

    thr = _key_to_float(lax.fori_loop(0, 32, value_bit, jnp.full(vec_shape, INT_MIN, I32)))
    need = kf - count(lambda kb, j: kb > thr)
    n_ge = count(lambda kb, j: kb >= thr)

    lim_ref[...] = jnp.full(lim_ref.shape, 2 ** idx_bits, I32)

    @pl.when(jnp.max(n_ge) > kf)
    def _():
        def index_bit(it, lim):
            cand = lim + (jnp.int32(1) << (idx_bits - 1 - it))
            n = count(lambda kb, j: jnp.where(kb == thr, idx(j), 2 ** idx_bits) < cand)
            return jnp.where(n <= need, cand, lim)
        lim = lax.fori_loop(0, idx_bits, index_bit, jnp.zeros(vec_shape, I32))
        lim_ref[...] = jnp.broadcast_to(lim, lim_ref.shape)

    lim = lim_ref[:, :1] if axis == 1 else lim_ref[:1, :]

    def write(j, c):
        kb = key_ref[at(j)]
        ok = jnp.where(kb > 0.5 * NEG, 0.0, NEG)
        tie = jnp.where(idx(j) < lim, ok, NEG)
        bias_ref[at(j)] = jnp.where(kb > thr, ok, jnp.where(kb == thr, tie, NEG))
        return c
    loop(write, 0)


def _indexer_scores(iq_ref, iw, ikb, nh, dk):
    acc = None
    for h in range(nh):
        d = lax.dot_general(iq_ref[:, h * dk:(h + 1) * dk], ikb, _NT, preferred_element_type=F32)
        term = jnp.maximum(d, 0.0) * iw[:, h:h + 1]
        acc = term if acc is None else acc + term
    return acc


def _select_prompt_kernel(iq_ref, iwt_ref, ik_ref, bias_ref, key_ref, lim_ref, *,
                          tq, tk, nh, dk, topk, nblk_total, idx_bits, chunk_shift):
    qi = pl.program_id(1)
    nkv = ((qi + 1) * tq + tk - 1) // tk

    def score_block(j, c):
        off = pl.multiple_of(j * tk, tk)
        ikb = ik_ref[pl.ds(off, tk), :]
        acc = None
        for h in range(nh):
            d = lax.dot_general(ikb, iq_ref[:, h * dk:(h + 1) * dk], _NT, preferred_element_type=F32)
            term = jnp.maximum(d, 0.0) * iwt_ref[h:h + 1, :]
            acc = term if acc is None else acc + term
        kpos = off + lax.broadcasted_iota(I32, (tk, tq), 0)
        qpos = qi * tq + lax.broadcasted_iota(I32, (tk, tq), 1)
        allowed = (kpos >> chunk_shift) <= (qpos >> chunk_shift)
        key_ref[pl.ds(off, tk), :] = jnp.where(allowed, acc, NEG)
        return c
    lax.fori_loop(0, nkv, score_block, 0)

    bias = bias_ref.at[0]
    _select_to_bias(key_ref, bias, lim_ref, nkv, tk, topk, idx_bits, axis=0)

    def fill(j, c):
        bias[pl.ds(pl.multiple_of(j * tk, tk), tk), :] = jnp.full((tk, tq), NEG, F32)
        return c
    lax.fori_loop(nkv, nblk_total, fill, 0)


def _select_prompt(iq, iwt, ik, nb, t, tq, tk, topk, chunk_shift):
    nh = iwt.shape[0]
    dk = ik.shape[1]
    nq = t // tq
    kern = functools.partial(
        _select_prompt_kernel, tq=tq, tk=tk, nh=nh, dk=dk, topk=topk, nblk_total=t // tk,
        idx_bits=int(t).bit_length(), chunk_shift=chunk_shift)
    return pl.pallas_call(
        kern,
        grid=(nb, nq),
        in_specs=[pl.BlockSpec((tq, nh * dk), lambda b, i: (b * nq + i, 0)),
                  pl.BlockSpec((nh, tq), lambda b, i: (0, b * nq + i)),
                  pl.BlockSpec((t, dk), lambda b, i: (b, 0))],
        out_specs=pl.BlockSpec((1, t, tq), lambda b, i: (b * nq + i, 0, 0)),
        out_shape=jax.ShapeDtypeStruct((nb * nq, t, tq), F32),
        scratch_shapes=[pltpu.VMEM((t, tq), F32), pltpu.VMEM((SUBLANES, tq), I32)],
        compiler_params=_params(2),
        name="select_prompt",
    )(iq, iwt, ik)


def _select_sample_kernel(iq_ref, iw_ref, ikc_ref, ikn_ref, bias_ref, key_ref, m_ref, *,
                          ts, past, sw, nh, dk, topk, idx_bits, chunk_shift):
    iw = iw_ref[...]
    for j in range(past // sw):
        ikb = ikc_ref[j * sw:(j + 1) * sw, :].astype(BF16)
        acc = _indexer_scores(iq_ref, iw, ikb, nh, dk)
        qpos = past + lax.broadcasted_iota(I32, (ts, sw), 0)
        kpos = j * sw + lax.broadcasted_iota(I32, (ts, sw), 1)
        allowed = (kpos >> chunk_shift) <= (qpos >> chunk_shift)
        key_ref[:, j * sw:(j + 1) * sw] = jnp.where(allowed, acc, NEG)
    key_ref[:, past:past + LANES] = jnp.full((ts, LANES), NEG, F32)
    acc = _indexer_scores(iq_ref, iw, ikn_ref[...], nh, dk)
    qpos = past + lax.broadcasted_iota(I32, (ts, ts), 0)
    kpos = past + lax.broadcasted_iota(I32, (ts, ts), 1)
    allowed = (kpos >> chunk_shift) <= (qpos >> chunk_shift)
    key_ref[:, past:past + ts] = jnp.where(allowed, acc, NEG)
    _select_to_bias(key_ref, bias_ref, m_ref, past // LANES + 1, LANES, topk, idx_bits, axis=1)


def _select_sample(iq, iw, ik_cache, ik_new, row_blk_off, layer, nb, ts, topk, chunk_shift):
    nh = iw.shape[1]
    past, dk = ik_cache.shape[2], ik_cache.shape[3]
    assert past % LANES == 0 and ts <= LANES
    s_pad = past + LANES
    kern = functools.partial(
        _select_sample_kernel, ts=ts, past=past, sw=_col_tile((past,), 512), nh=nh, dk=dk, topk=topk,
        idx_bits=int(s_pad).bit_length(), chunk_shift=chunk_shift)
    return pl.pallas_call(
        kern,
        grid=(nb,),
        in_specs=[pl.BlockSpec((ts, nh * dk), lambda b: (row_blk_off + b, 0)),
                  pl.BlockSpec((ts, nh), lambda b: (row_blk_off + b, 0)),
                  pl.BlockSpec((None, None, past, dk), lambda b: (layer, b, 0, 0)),
                  pl.BlockSpec((ts, dk), lambda b: (b, 0))],
        out_specs=pl.BlockSpec((ts, s_pad), lambda b: (b, 0)),
        out_shape=jax.ShapeDtypeStruct((nb * ts, s_pad), F32),
        scratch_shapes=[pltpu.VMEM((ts, s_pad), F32), pltpu.VMEM((ts, LANES), I32)],
        compiler_params=_params(1),
        name="select_sample",
    )(iq, iw, ik_cache, ik_new)


def _attn_prompt_kernel(q_ref, k_ref, vt_ref, bias_ref, o_ref, m_ref, l_ref, acc_ref, s_ref, mc_ref, *,
                        tq, tk, grp, hd, scale_log2e):
    nkv = ((pl.program_id(1) + 1) * tq + tk - 1) // tk
    m_ref[...] = jnp.full(m_ref.shape, -jnp.inf, F32)
    l_ref[...] = jnp.zeros(l_ref.shape, F32)
    acc_ref[...] = jnp.zeros(acc_ref.shape, F32)

    def body(j, c):
        off = pl.multiple_of(j * tk, tk)
        kb = k_ref[pl.ds(off, tk), :]
        vtb = vt_ref[:, pl.ds(off, tk)]
        bias = bias_ref[0, pl.ds(off, tk), :]
        for hh in range(grp):
            qh = q_ref[:, hh * hd:(hh + 1) * hd]
            s = lax.dot_general(kb, qh, _NT, preferred_element_type=F32) * scale_log2e + bias
            s_ref[hh] = s
            mc_ref[hh] = jnp.max(s, axis=0, keepdims=True)
        for hh in range(grp):
            s = s_ref[hh]
            m = m_ref[hh]
            m_new = jnp.maximum(m, mc_ref[hh])
            alpha = jnp.exp2(m - m_new)
            p = jnp.exp2(s - m_new)
            l_ref[hh] = alpha * l_ref[hh] + jnp.sum(p, axis=0, keepdims=True)
            acc_ref[hh] = alpha * acc_ref[hh] + jnp.dot(vtb, p.astype(BF16),
                                                        preferred_element_type=F32)
            m_ref[hh] = m_new
        return c
    lax.fori_loop(0, nkv, body, 0)

    for hh in range(grp):
        o_ref[:, hh * hd:(hh + 1) * hd] = (acc_ref[hh] / l_ref[hh]).T.astype(o_ref.dtype)


def _attn_prompt(q, k, vt, bias_t, nb, t, tq, tk, kvh, hd):
    aw = q.shape[1]
    grp = aw // (kvh * hd)
    nq = t // tq
    kern = functools.partial(_attn_prompt_kernel, tq=tq, tk=tk, grp=grp, hd=hd,
                             scale_log2e=hd ** -0.5 * math.log2(math.e))
    return pl.pallas_call(
        kern,
        grid=(nb, nq, kvh),
        in_specs=[pl.BlockSpec((tq, grp * hd), lambda b, i, g: (b * nq + i, g)),
                  pl.BlockSpec((t, hd), lambda b, i, g: (b, g)),
                  pl.BlockSpec((hd, t), lambda b, i, g: (g, b)),
                  pl.BlockSpec((1, t, tq), lambda b, i, g: (b * nq + i, 0, 0))],
        out_specs=pl.BlockSpec((tq, grp * hd), lambda b, i, g: (b * nq + i, g)),
        out_shape=jax.ShapeDtypeStruct((nb * t, aw), BF16),
        scratch_shapes=[pltpu.VMEM((grp, 1, tq), F32),
                        pltpu.VMEM((grp, 1, tq), F32),
                        pltpu.VMEM((grp, hd, tq), F32),
                        pltpu.VMEM((grp, tk, tq), F32),
                        pltpu.VMEM((grp, 1, tq), F32)],
        compiler_params=_params(3),
        name="attn_prompt",
    )(q, k, vt, bias_t)


def _attn_sample_kernel(q_ref, kc_ref, vc_ref, kn_ref, vn_ref, bias_ref, o_ref, *,
                        ts, past, kvh, grp, hd, scale):
    bias_c = bias_ref[:, :past]
    bias_n = bias_ref[:, past:past + ts]
    for g in range(kvh):
        kc = kc_ref[pl.ds(g, past, stride=kvh), :].astype(BF16)
        vc = vc_ref[pl.ds(g, past, stride=kvh), :].astype(BF16)
        kn = kn_ref[:, g * hd:(g + 1) * hd]
        vn = vn_ref[:, g * hd:(g + 1) * hd]
        for hh in range(g * grp, (g + 1) * grp):
            qh = q_ref[:, hh * hd:(hh + 1) * hd]
            sc = lax.dot_general(qh, kc, _NT, preferred_element_type=F32) * scale + bias_c
            sn = lax.dot_general(qh, kn, _NT, preferred_element_type=F32) * scale + bias_n
            m = jnp.maximum(jnp.max(sc, axis=-1, keepdims=True), jnp.max(sn, axis=-1, keepdims=True))
            pc = jnp.exp(sc - m)
            pn = jnp.exp(sn - m)
            l = jnp.sum(pc, axis=-1, keepdims=True) + jnp.sum(pn, axis=-1, keepdims=True)
            acc = (jnp.dot(pc.astype(BF16), vc, preferred_element_type=F32)
                   + jnp.dot(pn.astype(BF16), vn, preferred_element_type=F32))
            o_ref[:, hh * hd:(hh + 1) * hd] = (acc / l).astype(o_ref.dtype)


def _attn_sample(q, k_cache, v_cache, k_new, v_new, bias, row_blk_off, layer, nb, ts, kvh, hd):
    aw = q.shape[1]
    grp = aw // (kvh * hd)
    past = k_cache.shape[2]
    depth = k_cache.shape[0]
    k_cache = k_cache.reshape(depth, nb, past * kvh, hd)
    v_cache = v_cache.reshape(depth, nb, past * kvh, hd)
    cache_spec = lambda: pl.BlockSpec((None, None, past * kvh, hd), lambda b: (layer, b, 0, 0))
    kern = functools.partial(_attn_sample_kernel, ts=ts, past=past, kvh=kvh, grp=grp, hd=hd,
                             scale=hd ** -0.5)
    return pl.pallas_call(
        kern,
        grid=(nb,),
        in_specs=[pl.BlockSpec((ts, aw), lambda b: (row_blk_off + b, 0)),
                  cache_spec(), cache_spec(),
                  pl.BlockSpec((ts, kvh * hd), lambda b: (b, 0)),
                  pl.BlockSpec((ts, kvh * hd), lambda b: (b, 0)),
                  pl.BlockSpec((ts, past + LANES), lambda b: (b, 0))],
        out_specs=pl.BlockSpec((ts, aw), lambda b: (b, 0)),
        out_shape=jax.ShapeDtypeStruct((nb * ts, aw), BF16),
        compiler_params=_params(1),
        name="attn_sample",
    )(q, k_cache, v_cache, k_new, v_new, bias)


def _lru_kernel(xl_ref, yl_ref, c0_ref, h0_ref, cw_ref, cb_ref, wa_ref, ba_ref, wx_ref, bx_ref,
                lam_ref, o_ref, hl_ref, cn_ref, xbuf, hc, a_s, b_s, h_s, *, tt, cw, bw):
    ti = pl.program_id(2)
    pad = SUBLANES
    tail = CONV_W - 1

    @pl.when(ti == 0)
    def _():
        xbuf[pad - tail:pad, :] = c0_ref[0]
        hc[...] = jnp.broadcast_to(h0_ref[0], hc.shape)
        a_s[:pad, :] = jnp.ones((pad, cw), F32)
        b_s[:pad, :] = jnp.zeros((pad, cw), F32)

    x = xl_ref[...]
    xbuf[pad:, :] = x
    u = cb_ref[...]
    for j in range(CONV_W):
        u = u + xbuf[pad - tail + j:pad - tail + j + tt, :] * cw_ref[j:j + 1, :]
    xbuf[pad - tail:pad, :] = xbuf[pad + tt - tail:pad + tt, :]

    @pl.when(ti == pl.num_programs(2) - 1)
    def _():
        cn_ref[0] = x[tt - tail:, :]

    ub = u.astype(BF16)
    for n in range(cw // bw):
        cs = slice(n * bw, (n + 1) * bw)
        un = ub[:, cs]
        r = jax.nn.sigmoid(jnp.dot(un, wa_ref[n], preferred_element_type=F32) + ba_ref[:, cs])
        i = jax.nn.sigmoid(jnp.dot(un, wx_ref[n], preferred_element_type=F32) + bx_ref[:, cs])
        log_a = -LRU_C * r * jax.nn.softplus(-lam_ref[:, cs])
        a_s[pad:, cs] = jnp.exp(log_a)
        t = jnp.tanh(log_a)
        b_s[pad:, cs] = jnp.sqrt(-2.0 * t / (1.0 - t)) * (i * u[:, cs])

    ngrp = tt // SUBLANES
    row = lax.broadcasted_iota(I32, (ngrp, SUBLANES, cw), 1)
    a = a_s[pad:, :].reshape(ngrp, SUBLANES, cw)
    b = b_s[pad:, :].reshape(ngrp, SUBLANES, cw)
    d = 1
    while d < SUBLANES:
        keep = row >= d
        a_sh = jnp.where(keep, pltpu.roll(a, d, 1), 1.0)
        b_sh = jnp.where(keep, pltpu.roll(b, d, 1), 0.0)
        b = a * b_sh + b
        a = a * a_sh
        d *= 2
    a_s[pad:, :] = a.reshape(tt, cw)
    b_s[pad:, :] = b.reshape(tt, cw)

    carry = hc[...]
    for g in range(tt // SUBLANES):
        rs = slice(pad + g * SUBLANES, pad + (g + 1) * SUBLANES)
        h = a_s[rs, :] * carry + b_s[rs, :]
        h_s[g * SUBLANES:(g + 1) * SUBLANES, :] = h
        carry = jnp.broadcast_to(h[SUBLANES - 1:, :], hc.shape)
    hc[...] = carry

    @pl.when(ti == pl.num_programs(2) - 1)
    def _():
        hl_ref[0] = carry[:1, :]

    o_ref[...] = (h_s[...] * jax.nn.gelu(yl_ref[...])).astype(o_ref.dtype)


def _lru_branch(tail_proj, xl_col, yl_col, row_off, conv0, h0, p, nb, t):
    r = conv0.shape[-1]
    bw = p['lru_wa'].shape[-1]
    tt = t if t <= 256 else 256
    cw = bw * max(1, min(r // bw, (512 * 256 // tt) // bw))
    assert t % tt == 0 and r % cw == 0 and row_off % tt == 0 and tt % SUBLANES == 0
    nt = t // tt
    ro = row_off // tt
    xo, yo = xl_col // cw, yl_col // cw
    assert xl_col % cw == 0 and yl_col % cw == 0
    nblk = cw // bw
    kern = functools.partial(_lru_kernel, tt=tt, cw=cw, bw=bw)
    vec = lambda: pl.BlockSpec((1, cw), lambda b, c, i: (0, c))
    return pl.pallas_call(
        kern,
        grid=(nb, r // cw, nt),
        in_specs=[pl.BlockSpec((tt, cw), lambda b, c, i: (ro + b * nt + i, xo + c)),
                  pl.BlockSpec((tt, cw), lambda b, c, i: (ro + b * nt + i, yo + c)),
                  pl.BlockSpec((1, CONV_W - 1, cw), lambda b, c, i: (b, 0, c)),
                  pl.BlockSpec((1, 1, cw), lambda b, c, i: (b, 0, c)),
                  pl.BlockSpec((CONV_W, cw), lambda b, c, i: (0, c)),
                  vec(),
                  pl.BlockSpec((nblk, bw, bw), lambda b, c, i: (c, 0, 0)),
                  vec(),
                  pl.BlockSpec((nblk, bw, bw), lambda b, c, i: (c, 0, 0)),
                  vec(), vec()],
        out_specs=[pl.BlockSpec((tt, cw), lambda b, c, i: (b * nt + i, c)),
                   pl.BlockSpec((1, 1, cw), lambda b, c, i: (b, 0, c)),
                   pl.BlockSpec((1, CONV_W - 1, cw), lambda b, c, i: (b, 0, c))],
        out_shape=[jax.ShapeDtypeStruct((nb * t, r), BF16),
                   jax.ShapeDtypeStruct((nb, 1, r), F32),
                   jax.ShapeDtypeStruct((nb, CONV_W - 1, r), F32)],
        scratch_shapes=[pltpu.VMEM((tt + SUBLANES, cw), F32),
                        pltpu.VMEM((SUBLANES, cw), F32),
                        pltpu.VMEM((tt + SUBLANES, cw), F32),
                        pltpu.VMEM((tt + SUBLANES, cw), F32),
                        pltpu.VMEM((tt, cw), F32)],
        compiler_params=_params(3),
        name="conv_rglru",
    )(tail_proj, tail_proj, conv0, h0.reshape(nb, 1, r), p['conv_w'], p['conv_b'].reshape(1, r),
      p['wa_bf'], p['lru_ba'].reshape(1, r), p['wx_bf'], p['lru_bx'].reshape(1, r),
      p['lru_lambda'].reshape(1, r))


def _mix_kernel(oa_ref, ol_ref, w_ref, ga_ref, gl_ref, o_ref, w1b_ref, w2b_ref, *, ka):
    @pl.when(pl.program_id(1) == 0)
    def _():
        w1b_ref[...] = w_ref[:ka, :].astype(BF16)
        w2b_ref[...] = w_ref[ka:, :].astype(BF16)
    ya = jnp.dot(oa_ref[...], w1b_ref[...], preferred_element_type=F32)
    yl = jnp.dot(ol_ref[...], w2b_ref[...], preferred_element_type=F32)
    mixed = jax.nn.sigmoid(ga_ref[...]) * ya + jax.nn.sigmoid(gl_ref[...]) * yl
    o_ref[...] = mixed.astype(o_ref.dtype)


def _mix(oa, ol, w, layer, tail_proj, row0, ga_col, gl_col):
    m, ka = oa.shape
    kl = ol.shape[1]
    n = w.shape[2]
    tm = _row_tile(math.gcd(m, row0) if row0 else m, 512)
    ro = row0 // tm
    tn = _col_tile((n, ga_col, gl_col), 512)
    go, lo = ga_col // tn, gl_col // tn
    return pl.pallas_call(
        functools.partial(_mix_kernel, ka=ka),
        grid=(n // tn, m // tm),
        in_specs=[pl.BlockSpec((tm, ka), lambda j, i: (i, 0)),
                  pl.BlockSpec((tm, kl), lambda j, i: (i, 0)),
                  pl.BlockSpec((None, ka + kl, tn), lambda j, i: (layer, 0, j),
                               pipeline_mode=pl.Buffered(1)),
                  pl.BlockSpec((tm, tn), lambda j, i: (ro + i, go + j)),
                  pl.BlockSpec((tm, tn), lambda j, i: (ro + i, lo + j))],
        out_specs=pl.BlockSpec((tm, tn), lambda j, i: (i, j)),
        out_shape=jax.ShapeDtypeStruct((m, n), BF16),
        scratch_shapes=[pltpu.VMEM((ka, tn), BF16), pltpu.VMEM((kl, tn), BF16)],
        compiler_params=_params(2),
        name="branch_mix",
    )(oa, ol, w, tail_proj, tail_proj)


def _out_proj_kernel(a_ref, w_ref, x_ref, o_ref, wb_ref):
    _cast_weight_once(w_ref, wb_ref)
    o_ref[...] = x_ref[...] + jnp.dot(a_ref[...], wb_ref[...], preferred_element_type=F32)


def _out_proj(a, w, layer, x):
    m, k = a.shape
    n = w.shape[2]
    tm = _row_tile(m, 1024)
    tn = _col_tile((n,), 512)
    return pl.pallas_call(
        _out_proj_kernel,
        grid=(n // tn, m // tm),
        in_specs=[pl.BlockSpec((tm, k), lambda j, i: (i, 0)),
                  pl.BlockSpec((None, k, tn), lambda j, i: (layer, 0, j)),
                  pl.BlockSpec((tm, tn), lambda j, i: (i, j))],
        out_specs=pl.BlockSpec((tm, tn), lambda j, i: (i, j)),
        out_shape=jax.ShapeDtypeStruct((m, n), F32),
        scratch_shapes=[pltpu.VMEM((k, tn), BF16)],
        compiler_params=_params(2),
        name="out_proj",
    )(a, w, x)


def _residual_mm_kernel(a_ref, w_ref, x_ref, o_ref):
    o_ref[...] = x_ref[...] + jnp.dot(a_ref[...], w_ref[...], preferred_element_type=F32)


def _residual_mm(a, row0, w, x, tm_target, tn_target):
    k = a.shape[1]
    m = x.shape[0]
    n = w.shape[1]
    tm = _row_tile(math.gcd(m, row0) if row0 else m, tm_target)
    ro = row0 // tm
    tn = _col_tile((n,), tn_target)
    return pl.pallas_call(
        _residual_mm_kernel,
        grid=(m // tm, n // tn),
        in_specs=[pl.BlockSpec((tm, k), lambda i, j: (ro + i, 0)),
                  pl.BlockSpec((k, tn), lambda i, j: (0, j)),
                  pl.BlockSpec((tm, tn), lambda i, j: (i, j))],
        out_specs=pl.BlockSpec((tm, tn), lambda i, j: (i, j)),
        out_shape=jax.ShapeDtypeStruct((m, n), F32),
        compiler_params=_params(2),
        name="residual_matmul",
    )(a, w, x)


def _swiglu_kernel(a_ref, wg_ref, wu_ref, o_ref, wgb_ref, wub_ref):
    _cast_weight_once(wg_ref, wgb_ref)
    _cast_weight_once(wu_ref, wub_ref)
    a = a_ref[...]
    g = jnp.dot(a, wgb_ref[...], preferred_element_type=F32)
    u = jnp.dot(a, wub_ref[...], preferred_element_type=F32)
    o_ref[...] = (jax.nn.silu(g) * u).astype(o_ref.dtype)


def _swiglu(a, wg, wu, layer):
    m, k = a.shape
    n = wg.shape[2]
    tm = _row_tile(m, 1088)
    tn = _col_tile((n,), 256)
    wspec = lambda: pl.BlockSpec((None, k, tn), lambda j, i: (layer, 0, j))
    return pl.pallas_call(
        _swiglu_kernel,
        grid=(n // tn, m // tm),
        in_specs=[pl.BlockSpec((tm, k), lambda j, i: (i, 0)), wspec(), wspec()],
        out_specs=pl.BlockSpec((tm, tn), lambda j, i: (i, j)),
        out_shape=jax.ShapeDtypeStruct((m, n), BF16),
        scratch_shapes=[pltpu.VMEM((k, tn), BF16), pltpu.VMEM((k, tn), BF16)],
        compiler_params=_params(2),
        name="swiglu",
    )(a, wg, wu)


def _rope_tables(pos, hd):
    rot = hd // 4
    half = rot // 2
    inv = ROPE_THETA ** (-jnp.arange(half, dtype=F32) * 2.0 / rot)
    ang = pos.astype(F32)[:, None] * inv[None, :]
    cos, sin = jnp.cos(ang), jnp.sin(ang)
    n = pos.shape[0]
    c = jnp.concatenate([cos, cos, jnp.ones((n, hd - rot), F32)], axis=-1)
    s = jnp.concatenate([-sin, sin, jnp.zeros((n, hd - rot), F32)], axis=-1)
    return c, s


def _layer(xp, xs, dims, layer, caches, states, p):
    bp, tp, bs, ts, past, kvh, hd, di, ih = dims
    cache_k, cache_v, cache_ik = caches
    state_lru, state_conv = states
    d = xp.shape[1]
    mp, ms = bp * tp, bs * ts
    r = state_lru.shape[-1]
    aw = p['w_branch'].shape[1] - r
    kvw = kvh * hd
    assert hd == LANES and di == LANES and mp % ts == 0
    chunk_shift = CHUNK.bit_length() - 1
    assert 1 << chunk_shift == CHUNK
    rows_p, rows_s = (0, mp), (mp, ms)

    pos = jnp.concatenate([jnp.tile(jnp.arange(tp, dtype=I32), bp),
                           jnp.tile(past + jnp.arange(ts, dtype=I32), bs)])
    tabs = _rope_tables(pos, hd)

    xn = _rmsnorm(xp, xs, p['norm_mix'])
    tn = _col_tile((aw, kvw, ih * di), 512)
    c_k, c_v, c_iq = aw, aw + kvw, aw + 2 * kvw
    w_in = p['w_in_t']
    c_iw = c_iq + ih * di
    c_ik = c_iw + ih
    c_xl = c_ik + di
    (q_bf,) = _project(xn, w_in, layer, 0, aw, tn, (BF16,), gain=p['norm_q'], rope_tabs=tabs)
    (iq_bf,) = _project(xn, w_in, layer, c_iq, ih * di, tn, (BF16,), rope_tabs=tabs)
    (iw,) = _project(xn, w_in, layer, c_iw, ih, ih, (F32,), scale=float((ih * di) ** -0.5))
    tn_tail = _col_tile((r, d), 512)
    (tail,) = _project(xn, w_in, layer, c_xl, 2 * r + 2 * d, tn_tail, (F32,))
    xl_col, yl_col, ga_col, gl_col = 0, r, 2 * r, 2 * r + d
    kvi = []
    for rows in (rows_p, rows_s):
        k_f, k_bf = _project(xn, w_in, layer, c_k, kvw, tn, (F32, BF16), rows=rows,
                             gain=p['norm_k'], rope_tabs=tabs)
        v_f, v_bf = _project(xn, w_in, layer, c_v, kvw, tn, (F32, BF16), rows=rows)
        ik_f, ik_bf = _project(xn, w_in, layer, c_ik, di, di, (F32, BF16), rows=rows,
                               gain=p['norm_idx_k'], rope_tabs=tabs)
        kvi.append((k_f, k_bf, v_f, v_bf, ik_f, ik_bf))
    (kp_f, kp_bf, vp_f, vp_bf, ikp_f, ikp_bf), (ks_f, ks_bf, vs_f, vs_bf, iks_f, iks_bf) = kvi

    tq = 512 if tp % 512 == 0 else (256 if tp % 256 == 0 else LANES)
    tk = 512 if tp % 512 == 0 else tq
    bias_p = _select_prompt(iq_bf, iw.T, ikp_bf, bp, tp, tq, tk, min(TOPK_MAX, tp // 4), chunk_shift)
    oa_p = _attn_prompt(q_bf, kp_bf, vp_bf.T, bias_p, bp, tp, tq, tk, kvh, hd)
    sblk = mp // ts
    bias_s = _select_sample(iq_bf, iw, cache_ik, iks_bf, sblk, layer, bs, ts,
                            min(TOPK_MAX, (past + ts) // 4), chunk_shift)
    oa_s = _attn_sample(q_bf, cache_k, cache_v, ks_bf, vs_bf, bias_s, sblk, layer, bs, ts, kvh, hd)

    ol_p, h_p, c_p = _lru_branch(tail, xl_col, yl_col, 0, jnp.zeros((bp, CONV_W - 1, r), F32),
                                 jnp.zeros((bp, r), F32), p, bp, tp)
    ol_s, h_s, c_s = _lru_branch(tail, xl_col, yl_col, mp, state_conv, state_lru, p, bs, ts)

    x1 = []
    for oa, ol, x, row0 in ((oa_p, ol_p, xp, 0), (oa_s, ol_s, xs, mp)):
        mixed = _mix(oa, ol, p['w_branch'], layer, tail, row0, ga_col, gl_col)
        x1.append(_out_proj(mixed, p['w_out'], layer, x))
    xn2 = _rmsnorm(x1[0], x1[1], p['norm_ffn'])
    hff = _swiglu(xn2, p['w_gate'], p['w_up'], layer)
    yp = _residual_mm(hff, 0, p['w_down_bf'], x1[0], 512, 256)
    ys = _residual_mm(hff, mp, p['w_down_bf'], x1[1], 512, 256)

    new_p = (kp_f.reshape(bp, tp, kvh, hd), vp_f.reshape(bp, tp, kvh, hd),
             ikp_f.reshape(bp, tp, di), h_p.reshape(bp, r), c_p)
    new_s = (ks_f.reshape(bs, ts, kvh, hd), vs_f.reshape(bs, ts, kvh, hd),
             iks_f.reshape(bs, ts, di), h_s.reshape(bs, r), c_s)
    return yp, ys, new_p, new_s


def kernel(x_prompt, x_sample, cache_k, cache_v, cache_idx_k, state_lru, state_conv, norm_mix, w_in,
           norm_q, norm_k, norm_idx_k, conv_w, conv_b, lru_wa, lru_ba, lru_wx, lru_bx, lru_lambda,
           w_branch, w_out, norm_ffn, w_gate, w_up, w_down):
    depth = w_in.shape[0]
    bp, tp, d = x_prompt.shape
    bs, ts, _ = x_sample.shape
    past, kvh, hd = cache_k.shape[2], cache_k.shape[3], cache_k.shape[4]
    di = cache_idx_k.shape[-1]
    r = state_lru.shape[-1]
    aw = w_branch.shape[1] - r
    kvw = kvh * hd
    ih = (w_in.shape[-1] - aw - 2 * kvw - di - 2 * r - 2 * d) // (di + 1)
    dims =(bp, tp, bs, ts, past, kvh, hd, di, ih)
    xp, xs = x_prompt.reshape(bp * tp, d), x_sample.reshape(bs * ts, d)
    caches = (cache_k, cache_v, cache_idx_k)
    news_p, news_s = [], []
    for l in range(depth):
        p = {
            'norm_mix': norm_mix[l], 'norm_q': norm_q[l], 'norm_k': norm_k[l],
            'norm_idx_k': norm_idx_k[l], 'norm_ffn': norm_ffn[l],
            'w_in_t': jnp.swapaxes(w_in, 1, 2),
            'conv_w': conv_w[l], 'conv_b': conv_b[l],
            'lru_wa': lru_wa[l], 'wa_bf': lru_wa[l].astype(BF16), 'wx_bf': lru_wx[l].astype(BF16),
            'lru_ba': lru_ba[l], 'lru_bx': lru_bx[l], 'lru_lambda': lru_lambda[l],
            'w_branch': w_branch, 'w_out': w_out, 'w_gate': w_gate, 'w_up': w_up,
            'w_down_bf': w_down[l].astype(BF16),
        }
        xp, xs, new_p, new_s = _layer(xp, xs, dims, l, caches, (state_lru[l], state_conv[l]), p)
        news_p.append(new_p)
        news_s.append(new_s)

    stack = lambda news, i: jnp.stack([n[i] for n in news])
    return (xp.reshape(bp, tp, d), xs.reshape(bs, ts, d),
            stack(news_p, 0), stack(news_p, 1), stack(news_p, 2), stack(news_p, 3), stack(news_p, 4),
            stack(news_s, 0), stack(news_s, 1), stack(news_s, 2), stack(news_s, 3), stack(news_s, 4))
```
